```python
import math
import jax, jax.numpy as jnp
from jax import lax
import numpy as np

D_MODEL = 2048
BATCH = 1
SEQ = 8192
DEPTH = 4
DEC_BATCH = 8
DEC_SEQ = 16
PAST_LEN = 2048

CHUNK = 64
N_MIXERS = 4
D_GROUP = D_MODEL // N_MIXERS
SSM_HEADS = 8
SSM_HEAD_DIM = D_GROUP // SSM_HEADS
SSM_GROUPS = 2
SSM_STATE = 128
SSM_CONV = 4
SSM_CONV_CH = D_GROUP + 2 * SSM_GROUPS * SSM_STATE
DA_HEADS = 4
DA_DIM = 64
DA_VDIM = 2 * DA_DIM
ROPE_DIM = DA_DIM // 4
ROPE_THETA = 500000.0
Q_BLOCK = 128
SG_CHUNK = 128
SG_GROUPS = 4
SG_CH = D_GROUP // SG_GROUPS
CV_WIDTH = 31
N_EGROUPS = 4
EXP_PER_GROUP = 4
N_EXPERTS = N_EGROUPS * EXP_PER_GROUP
D_EXPERT = 512
TOP_K = 2
ALPHA = (2 * DEPTH) ** 0.25
BETA = (8 * DEPTH) ** -0.25
EPS = 1e-5
PROJ_SPLITS = (D_GROUP, SSM_CONV_CH, SSM_HEADS,
               2 * DA_HEADS * DA_DIM, 2 * DA_HEADS * DA_DIM, DA_HEADS * DA_VDIM,
               D_GROUP, D_GROUP,
               D_GROUP, D_GROUP)
D_IN_PROJ = sum(PROJ_SPLITS)

kernel_name = 'hybrid_streaming_encoder_step'


def layer_norm(x, g, b):
    xf = x.astype(jnp.float32)
    mu = xf.mean(-1, keepdims=True)
    var = jnp.square(xf - mu).mean(-1, keepdims=True)
    return ((xf - mu) * lax.rsqrt(var + EPS) * g.astype(jnp.float32) + b.astype(jnp.float32)).astype(x.dtype)


def rms_norm(x, g):
    xf = x.astype(jnp.float32)
    return (xf * lax.rsqrt(jnp.mean(xf * xf, -1, keepdims=True) + EPS) * g.astype(jnp.float32)).astype(x.dtype)


def split_proj(h):
    pts, acc = [], 0
    for s in PROJ_SPLITS[:-1]:
        acc += s
        pts.append(acc)
    return jnp.split(h, pts, axis=-1)


def causal_dwconv(x, hist, w, b):
    c = x.shape[-1]
    xp = jnp.concatenate([hist, x], axis=1)
    y = lax.conv_general_dilated(xp, w[:, None, :].astype(xp.dtype), window_strides=(1,), padding='VALID',
                                 dimension_numbers=('NWC', 'WIO', 'NWC'), feature_group_count=c)
    return y + b, xp[:, x.shape[1]:]


def ssd_scan(x, dt, a, bm, cm, h0):
    b, L = x.shape[:2]
    q = min(CHUNK, L)
    nc = L // q
    hg = SSM_HEADS // SSM_GROUPS
    xf = x.astype(jnp.float32).reshape(b, nc, q, SSM_GROUPS, hg, SSM_HEAD_DIM)
    dtc = dt.reshape(b, nc, q, SSM_GROUPS, hg)
    bc = bm.astype(jnp.float32).reshape(b, nc, q, SSM_GROUPS, SSM_STATE)
    cc = cm.astype(jnp.float32).reshape(b, nc, q, SSM_GROUPS, SSM_STATE)
    cs = jnp.cumsum(dtc * a.reshape(SSM_GROUPS, hg), axis=2)
    seg = cs[:, :, :, None] - cs[:, :, None]
    causal = jnp.tril(jnp.ones((q, q), bool))[:, :, None, None]
    lmat = jnp.exp(jnp.where(causal, seg, -jnp.inf))
    cb = jnp.einsum('bclgn,bcsgn->bclsg', cc, bc)
    m = cb[..., None] * lmat * dtc[:, :, None]
    y_diag = jnp.einsum('bclsgh,bcsghp->bclghp', m, xf)
    w_s = jnp.exp(cs[:, :, -1:] - cs) * dtc
    st = jnp.einsum('bclgn,bclghp->bcghpn', bc, xf * w_s[..., None])
    decay = jnp.exp(cs[:, :, -1])

    def step(h, inp):
        s_c, d_c = inp
        return d_c[..., None, None] * h + s_c, h

    h_init = h0.astype(jnp.float32).reshape(b, SSM_GROUPS, hg, SSM_HEAD_DIM, SSM_STATE)
    h_last, h_in = lax.scan(step, h_init, (st.swapaxes(0, 1), decay.swapaxes(0, 1)))
    h_in = h_in.swapaxes(0, 1)
    y_off = jnp.einsum('bclgn,bcghpn->bclghp', cc, h_in) * jnp.exp(cs)[..., None]
    y = (y_diag + y_off).reshape(b, L, SSM_HEADS, SSM_HEAD_DIM)
    return y, h_last.reshape(b, SSM_HEADS, SSM_HEAD_DIM, SSM_STATE)


def ssd_mixer(z, xbc, dt_raw, conv_hist, h0, conv_w, conv_b, dt_bias, a_log, d_skip, norm_g):
    b, L = z.shape[:2]
    xbc, conv_new = causal_dwconv(xbc, conv_hist, conv_w, conv_b)
    xbc = jax.nn.silu(xbc)
    xs, bm, cm = jnp.split(xbc, [D_GROUP, D_GROUP + SSM_GROUPS * SSM_STATE], axis=-1)
    xs = xs.reshape(b, L, SSM_HEADS, SSM_HEAD_DIM)
    bm = bm.reshape(b, L, SSM_GROUPS, SSM_STATE)
    cm = cm.reshape(b, L, SSM_GROUPS, SSM_STATE)
    dt = jax.nn.softplus(dt_raw.astype(jnp.float32) + dt_bias.astype(jnp.float32))
    a = -jnp.exp(a_log.astype(jnp.float32))
    y, h_last = ssd_scan(xs, dt, a, bm, cm, h0)
    y = y + d_skip.astype(jnp.float32)[:, None] * xs.astype(jnp.float32)
    y = y.reshape(b, L, D_GROUP) * jax.nn.silu(z.astype(jnp.float32))
    yg = y.reshape(b, L, SSM_GROUPS, D_GROUP // SSM_GROUPS)
    yg = yg * lax.rsqrt(jnp.mean(yg * yg, -1, keepdims=True) + EPS)
    y = yg.reshape(b, L, D_GROUP) * norm_g.astype(jnp.float32)
    return y.astype(z.dtype), conv_new, h_last.astype(z.dtype)


def rope(t, pos):
    half = ROPE_DIM // 2
    inv_freq = 1.0 / (ROPE_THETA ** (jnp.arange(half, dtype=jnp.float32) * 2.0 / ROPE_DIM))
    ang = pos.astype(jnp.float32)[:, None] * inv_freq
    cos = jnp.cos(ang)[None, :, None, None, :]
    sin = jnp.sin(ang)[None, :, None, None, :]
    tr = t[..., :ROPE_DIM].astype(jnp.float32)
    t1, t2 = tr[..., :half], tr[..., half:]
    rot = jnp.concatenate([t1 * cos - t2 * sin, t2 * cos + t1 * sin], -1)
    return jnp.concatenate([rot.astype(t.dtype), t[..., ROPE_DIM:]], -1)


def diff_lambda(lq1, lk1, lq2, lk2, lam_init):
    f32 = jnp.float32
    return (jnp.exp(jnp.sum(lq1.astype(f32) * lk1.astype(f32)))
            - jnp.exp(jnp.sum(lq2.astype(f32) * lk2.astype(f32))) + lam_init)


def diff_attn_core(q, k, v, lam, mask):
    s = jnp.einsum('bqhcd,bkhcd->bhcqk', q, k).astype(jnp.float32) * DA_DIM ** -0.5
    if mask is not None:
        s = jnp.where(mask, s, -jnp.inf)
    p = jax.nn.softmax(s, axis=-1)
    a = p[:, :, 0] - lam * p[:, :, 1]
    return jnp.einsum('bhqk,bkhe->bqhe', a.astype(v.dtype), v)


def diff_attn_prompt(q, k, v, lam):
    b, s = q.shape[:2]
    nb = s // Q_BLOCK
    qb = q.reshape(b, nb, Q_BLOCK, DA_HEADS, 2, DA_DIM).swapaxes(0, 1)
    key_chunk = jnp.arange(s) // CHUNK

    def one(args):
        qi, i = args
        q_chunk = (i * Q_BLOCK + jnp.arange(Q_BLOCK)) // CHUNK
        return diff_attn_core(qi, k, v, lam, key_chunk[None, :] <= q_chunk[:, None])

    o = lax.map(one, (qb, jnp.arange(nb)))
    return o.swapaxes(0, 1).reshape(b, s, DA_HEADS, DA_VDIM)


def sgu_mixer(u, v, ln_g, ln_b, w_s, b_s):
    v = layer_norm(v, ln_g, ln_b)
    b, L = v.shape[:2]
    q = min(SG_CHUNK, L)
    nc = L // q
    w = jnp.where(jnp.tril(jnp.ones((q, q), bool)), w_s[:, :q, :q], 0.0)
    vc = v.reshape(b, nc, q, SG_GROUPS, SG_CH)
    s = jnp.einsum('gts,bnsgc->bntgc', w, vc) + b_s[:, :q].T[None, None, :, :, None]
    return u * s.reshape(b, L, D_GROUP), v


def conformer_conv(a, gate, hist, w, bconv, ln_g, ln_b):
    glu = a * jax.nn.sigmoid(gate)
    y, hist_new = causal_dwconv(glu, hist, w, bconv)
    return jax.nn.silu(layer_norm(y, ln_g, ln_b)), hist_new


def hier_moe(x, wg_group, bg_group, wg_exp, bg_exp, w1, w3, w2):
    b, L, d = x.shape
    xt = x.reshape(b * L, d)
    p_group = jax.nn.softmax((xt @ wg_group).astype(jnp.float32) + bg_group.astype(jnp.float32), axis=-1)
    p_g, g_idx = lax.top_k(p_group, 1)
    le = ((xt @ wg_exp).astype(jnp.float32) + bg_exp.astype(jnp.float32)).reshape(-1, N_EGROUPS, EXP_PER_GROUP)
    le = jnp.einsum('tge,tg->te', le, jax.nn.one_hot(g_idx[:, 0], N_EGROUPS, dtype=jnp.float32))
    p_e, e_idx = lax.top_k(jax.nn.softmax(le, axis=-1), TOP_K)
    gate = p_g * p_e / p_e.sum(-1, keepdims=True)
    onehot = jax.nn.one_hot(g_idx * EXP_PER_GROUP + e_idx, N_EXPERTS, dtype=jnp.float32)
    combine = jnp.einsum('tk,tke->te', gate, onehot)
    h = jax.nn.silu(jnp.einsum('td,edf->tef', xt, w1)) * jnp.einsum('td,edf->tef', xt, w3)
    y = jnp.einsum('tef,efd->td', h * combine[:, :, None].astype(h.dtype), w2)
    return y.reshape(b, L, d)


def layer_forward(x, p, lam_init, start, ssm_hist, ssm_h0, cv_hist, past_k, past_v):
    b, L, _ = x.shape
    z, xbc, dt_raw, q, k, v, su, sv, ca, cg = split_proj(x @ p['w_in'])
    y_a, ssm_hist_new, ssm_h = ssd_mixer(z, xbc, dt_raw, ssm_hist, ssm_h0, p['ssm_conv_w'], p['ssm_conv_b'],
                                         p['ssm_dt_bias'], p['ssm_a_log'], p['ssm_d'], p['ssm_norm_g'])
    pos = start + jnp.arange(L)
    q = rope(q.reshape(b, L, DA_HEADS, 2, DA_DIM), pos)
    k = rope(k.reshape(b, L, DA_HEADS, 2, DA_DIM), pos)
    v = v.reshape(b, L, DA_HEADS, DA_VDIM)
    lam = diff_lambda(p['da_lq1'], p['da_lk1'], p['da_lq2'], p['da_lk2'], lam_init)
    if past_k is None:
        o = diff_attn_prompt(q, k, v, lam)
    else:
        o = diff_attn_core(q, jnp.concatenate([past_k, k], 1), jnp.concatenate([past_v, v], 1), lam, None)
    y_b = (rms_norm(o, p['da_norm_g']) * (1.0 - lam_init)).reshape(b, L, D_GROUP)
    y_c, v_rows = sgu_mixer(su, sv, p['sg_ln_g'], p['sg_ln_b'], p['sg_w'], p['sg_b'])
    y_d, cv_hist_new = conformer_conv(ca, cg, cv_hist, p['cv_w'], p['cv_b'], p['cv_ln_g'], p['cv_ln_b'])
    mix = jnp.concatenate([y_a, y_b, y_c, y_d], -1) @ p['w_out']
    x = layer_norm(ALPHA * x + mix, p['ln1_g'], p['ln1_b'])
    ffn = hier_moe(x, p['moe_wg_group'], p['moe_bg_group'], p['moe_wg_exp'], p['moe_bg_exp'],
                   p['moe_w1'], p['moe_w3'], p['moe_w2'])
    x = layer_norm(ALPHA * x + ffn, p['ln2_g'], p['ln2_b'])
    return x, ssm_hist_new, ssm_h, k, v, cv_hist_new, v_rows


def setup_inputs(seed: int = 0) -> dict:
    key = jax.random.key(seed)
    ks = iter(jax.random.split(key, 64))

    def nrm(shape, scale=1.0):
        return scale * jax.random.normal(next(ks), shape, jnp.float32)

    def gain(shape):
        return 1.0 + 0.02 * jax.random.normal(next(ks), shape, jnp.float32)

    dt = jnp.exp(jax.random.uniform(next(ks), (DEPTH, SSM_HEADS), jnp.float32, math.log(1e-3), math.log(1e-1)))
    a_init = jax.random.uniform(next(ks), (DEPTH, SSM_HEADS), jnp.float32, 1.0, 16.0)
    return {
        'x_prompt': nrm((BATCH, SEQ, D_MODEL)),
        'x_sample': nrm((DEC_BATCH, DEC_SEQ, D_MODEL)),
        'state_ssm_conv': nrm((DEPTH, DEC_BATCH, SSM_CONV - 1, SSM_CONV_CH)),
        'state_ssm': nrm((DEPTH, DEC_BATCH, SSM_HEADS, SSM_HEAD_DIM, SSM_STATE), 0.5),
        'cache_k': nrm((DEPTH, DEC_BATCH, PAST_LEN, DA_HEADS, 2, DA_DIM)),
        'cache_v': nrm((DEPTH, DEC_BATCH, PAST_LEN, DA_HEADS, DA_VDIM)),
        'state_conv': nrm((DEPTH, DEC_BATCH, CV_WIDTH - 1, D_GROUP)),
        'ln_in_g': gain((D_MODEL,)),
        'ln_in_b': nrm((D_MODEL,), 0.02),
        'w_in': nrm((DEPTH, D_MODEL, D_IN_PROJ), D_MODEL ** -0.5),
        'ssm_conv_w': nrm((DEPTH, SSM_CONV, SSM_CONV_CH), SSM_CONV ** -0.5),
        'ssm_conv_b': nrm((DEPTH, SSM_CONV_CH), 0.02),
        'ssm_dt_bias': dt + jnp.log(-jnp.expm1(-dt)),
        'ssm_a_log': jnp.log(a_init),
        'ssm_d': gain((DEPTH, SSM_HEADS)),
        'ssm_norm_g': gain((DEPTH, D_GROUP)),
        'da_lq1': nrm((DEPTH, DA_DIM), 0.1),
        'da_lk1': nrm((DEPTH, DA_DIM), 0.1),
        'da_lq2': nrm((DEPTH, DA_DIM), 0.1),
        'da_lk2': nrm((DEPTH, DA_DIM), 0.1),
        'da_norm_g': gain((DEPTH, DA_VDIM)),
        'sg_ln_g': gain((DEPTH, D_GROUP)),
        'sg_ln_b': nrm((DEPTH, D_GROUP), 0.02),
        'sg_w': nrm((DEPTH, SG_GROUPS, SG_CHUNK, SG_CHUNK), SG_CHUNK ** -0.5),
        'sg_b': gain((DEPTH, SG_GROUPS, SG_CHUNK)),
        'cv_w': nrm((DEPTH, CV_WIDTH, D_GROUP), CV_WIDTH ** -0.5),
        'cv_b': nrm((DEPTH, D_GROUP), 0.02),
        'cv_ln_g': gain((DEPTH, D_GROUP)),
        'cv_ln_b': nrm((DEPTH, D_GROUP), 0.02),
        'w_out': nrm((DEPTH, N_MIXERS * D_GROUP, D_MODEL), BETA * (N_MIXERS * D_GROUP) ** -0.5),
        'ln1_g': gain((DEPTH, D_MODEL)),
        'ln1_b': nrm((DEPTH, D_MODEL), 0.02),
        'moe_wg_group': nrm((DEPTH, D_MODEL, N_EGROUPS), D_MODEL ** -0.5),
        'moe_bg_group': nrm((DEPTH, N_EGROUPS), 0.01),
        'moe_wg_exp': nrm((DEPTH, D_MODEL, N_EXPERTS), D_MODEL ** -0.5),
        'moe_bg_exp': nrm((DEPTH, N_EXPERTS), 0.01),
        'moe_w1': nrm((DEPTH, N_EXPERTS, D_MODEL, D_EXPERT), D_MODEL ** -0.5),
        'moe_w3': nrm((DEPTH, N_EXPERTS, D_MODEL, D_EXPERT), D_MODEL ** -0.5),
        'moe_w2': nrm((DEPTH, N_EXPERTS, D_EXPERT, D_MODEL), BETA * D_EXPERT ** -0.5),
        'ln2_g': gain((DEPTH, D_MODEL)),
        'ln2_b': nrm((DEPTH, D_MODEL), 0.02),
    }


def reference(x_prompt, x_sample, state_ssm_conv, state_ssm, cache_k, cache_v, state_conv,
              ln_in_g, ln_in_b, w_in, ssm_conv_w, ssm_conv_b, ssm_dt_bias, ssm_a_log, ssm_d, ssm_norm_g,
              da_lq1, da_lk1, da_lq2, da_lk2, da_norm_g, sg_ln_g, sg_ln_b, sg_w, sg_b,
              cv_w, cv_b, cv_ln_g, cv_ln_b, w_out, ln1_g, ln1_b,
              moe_wg_group, moe_bg_group, moe_wg_exp, moe_bg_exp, moe_w1, moe_w3, moe_w2, ln2_g, ln2_b):
    stacked = {'w_in': w_in, 'ssm_conv_w': ssm_conv_w, 'ssm_conv_b': ssm_conv_b, 'ssm_dt_bias': ssm_dt_bias,
               'ssm_a_log': ssm_a_log, 'ssm_d': ssm_d, 'ssm_norm_g': ssm_norm_g,
               'da_lq1': da_lq1, 'da_lk1': da_lk1, 'da_lq2': da_lq2, 'da_lk2': da_lk2, 'da_norm_g': da_norm_g,
               'sg_ln_g': sg_ln_g, 'sg_ln_b': sg_ln_b, 'sg_w': sg_w, 'sg_b': sg_b,
               'cv_w': cv_w, 'cv_b': cv_b, 'cv_ln_g': cv_ln_g, 'cv_ln_b': cv_ln_b,
               'w_out': w_out, 'ln1_g': ln1_g, 'ln1_b': ln1_b,
               'moe_wg_group': moe_wg_group, 'moe_bg_group': moe_bg_group, 'moe_wg_exp': moe_wg_exp,
               'moe_bg_exp': moe_bg_exp, 'moe_w1': moe_w1, 'moe_w3': moe_w3, 'moe_w2': moe_w2,
               'ln2_g': ln2_g, 'ln2_b': ln2_b}
    bp = x_prompt.shape[0]
    dtype = x_prompt.dtype
    zero_ssm_hist = jnp.zeros((bp, SSM_CONV - 1, SSM_CONV_CH), dtype)
    zero_ssm_h = jnp.zeros((bp, SSM_HEADS, SSM_HEAD_DIM, SSM_STATE), dtype)
    zero_cv_hist = jnp.zeros((bp, CV_WIDTH - 1, D_GROUP), dtype)
    start_sample = cache_k.shape[2]
    xp = layer_norm(x_prompt, ln_in_g, ln_in_b)
    xs = layer_norm(x_sample, ln_in_g, ln_in_b)
    p_sc, p_ss, p_k, p_v, p_cv = [], [], [], [], []
    s_sc, s_ss, s_k, s_v, s_cv, s_sg = [], [], [], [], [], []
    for l in range(DEPTH):
        p = {name: arr[l] for name, arr in stacked.items()}
        lam_init = 0.8 - 0.6 * math.exp(-0.3 * l)
        xp, sc, ss, kk, vv, cv, _ = layer_forward(xp, p, lam_init, 0, zero_ssm_hist, zero_ssm_h, zero_cv_hist, None, None)
        p_sc.append(sc); p_ss.append(ss); p_k.append(kk); p_v.append(vv); p_cv.append(cv)
        xs, sc, ss, kk, vv, cv, sg = layer_forward(xs, p, lam_init, start_sample, state_ssm_conv[l], state_ssm[l],
                                                   state_conv[l], cache_k[l], cache_v[l])
        s_sc.append(sc); s_ss.append(ss); s_k.append(kk); s_v.append(vv); s_cv.append(cv); s_sg.append(sg)
    return (xp, xs,
            jnp.stack(p_sc), jnp.stack(p_ss), jnp.stack(p_k), jnp.stack(p_v), jnp.stack(p_cv),
            jnp.stack(s_sc), jnp.stack(s_ss), jnp.stack(s_k), jnp.stack(s_v), jnp.stack(s_cv), jnp.stack(s_sg))
```

```python
import functools
import math

import jax
import jax.numpy as jnp
from jax import lax
from jax.experimental import pallas as pl
from jax.experimental.pallas import tpu as pltpu

F32 = jnp.float32
BF16 = jnp.bfloat16
HI = lax.Precision.HIGHEST

D_MODEL = 2048
DEPTH = 4
CHUNK = 64
D_GROUP = 512
SSM_HEADS = 8
SSM_HEAD_DIM = 64
SSM_GROUPS = 2
SSM_STATE = 128
SSM_CONV = 4
SSM_CONV_CH = 1024
DA_HEADS = 4
DA_DIM = 64
DA_VDIM = 128
ROPE_DIM = 16
ROPE_THETA = 500000.0
SG_CHUNK = 128
SG_GROUPS = 4
CV_WIDTH = 31
N_EGROUPS = 4
EXP_PER_GROUP = 4
N_EXPERTS = 16
D_EXPERT = 512
ALPHA = (2 * DEPTH) ** 0.25
EPS = 1e-5

LANES = 128
SUBLANES = 8
VMEM_LIMIT = 56 * 1024 * 1024
D_MAIN = 10 * D_GROUP
MOE_TM = 256
NT = (((1,), (1,)), ((), ()))
TN = (((0,), (0,)), ((), ()))

CB_Z, CB_XS, CB_BC, CB_Q, CB_K, CB_V, CB_SU, CB_SV, CB_CA, CB_CG = range(10)


def _cp(sem):
    return pltpu.CompilerParams(dimension_semantics=sem, vmem_limit_bytes=VMEM_LIMIT)


def _pick(n, cands):
    for c in cands:
        if n % c == 0:
            return c
    raise ValueError(f"no tile for {n}")


def _ln(x, g, b):
    mu = jnp.mean(x, -1, keepdims=True)
    xc = x - mu
    var = jnp.mean(xc * xc, -1, keepdims=True)
    return xc * lax.rsqrt(var + EPS) * g + b


def _silu(x):
    return x * jax.nn.sigmoid(x)


def _softplus(x):
    return jnp.maximum(x, 0.0) + jnp.log1p(jnp.exp(-jnp.abs(x)))


def _ln_in_kernel(x_ref, g_ref, b_ref, xf_ref, xb_ref):
    y = _ln(x_ref[...], g_ref[...], b_ref[...])
    xf_ref[...] = y
    xb_ref[...] = y.astype(BF16)


def _ln_in(x, g, b):
    t = x.shape[0]
    tm = _pick(t, (640, 512, 128))
    return pl.pallas_call(
        _ln_in_kernel,
        grid=(t // tm,),
        in_specs=[pl.BlockSpec((tm, D_MODEL), lambda i: (i, 0)),
                  pl.BlockSpec((1, D_MODEL), lambda i: (0, 0)),
                  pl.BlockSpec((1, D_MODEL), lambda i: (0, 0))],
        out_specs=[pl.BlockSpec((tm, D_MODEL), lambda i: (i, 0)),
                   pl.BlockSpec((tm, D_MODEL), lambda i: (i, 0))],
        out_shape=[jax.ShapeDtypeStruct((t, D_MODEL), F32), jax.ShapeDtypeStruct((t, D_MODEL), BF16)],
        compiler_params=_cp(("parallel",)),
        name="ln_in",
    )(x, g.reshape(1, -1), b.reshape(1, -1))


def _inproj_kernel(x_ref, w_ref, wdt_ref, h_ref, dt_ref):
    x = x_ref[...]
    h_ref[...] = jnp.dot(x, w_ref[...], preferred_element_type=F32)

    @pl.when(pl.program_id(1) == 0)
    def _():
        dt_ref[...] = jnp.dot(x, wdt_ref[...], preferred_element_type=F32)


def _inproj(xb, w_main, w_dt):
    t = xb.shape[0]
    tm = _pick(t, (1664, 1024, 128))
    tn = D_GROUP
    return pl.pallas_call(
        _inproj_kernel,
        grid=(t // tm, D_MAIN // tn),
        in_specs=[pl.BlockSpec((tm, D_MODEL), lambda i, j: (i, 0)),
                  pl.BlockSpec((D_MODEL, tn), lambda i, j: (0, j)),
                  pl.BlockSpec((D_MODEL, LANES), lambda i, j: (0, 0))],
        out_specs=[pl.BlockSpec((tm, tn), lambda i, j: (i, j)),
                   pl.BlockSpec((tm, LANES), lambda i, j: (i, 0))],
        out_shape=[jax.ShapeDtypeStruct((t, D_MAIN), F32), jax.ShapeDtypeStruct((t, LANES), F32)],
        compiler_params=_cp(("parallel", "arbitrary")),
        name="inproj",
    )(xb, w_main, w_dt)


def _rope_kernel(q_ref, k_ref, v_ref, c_ref, s1_ref, s2_ref, qb_ref, kf_ref, kb_ref, vb_ref):
    c = c_ref[...]
    s1 = s1_ref[...]
    s2 = s2_ref[...]

    def rot(x):
        return x * c + pltpu.roll(x, LANES - ROPE_DIM // 2, 1) * s1 + pltpu.roll(x, ROPE_DIM // 2, 1) * s2

    for h in range(DA_HEADS):
        sl = slice(h * LANES, (h + 1) * LANES)
        qr = rot(q_ref[:, sl])
        kr = rot(k_ref[:, sl])
        qb_ref[:, sl] = (qr * (DA_DIM ** -0.5)).astype(BF16)
        kf_ref[:, sl] = kr
        kb_ref[:, sl] = kr.astype(BF16)
    vb_ref[...] = v_ref[...].astype(BF16)


def _rope(h, cos_t, sin1_t, sin2_t):
    t = h.shape[0]
    tm = _pick(t, (640, 512, 128))
    hspec = lambda cb: pl.BlockSpec((tm, D_GROUP), lambda i, cb=cb: (i, cb))
    tspec = pl.BlockSpec((tm, LANES), lambda i: (i, 0))
    ospec = pl.BlockSpec((tm, D_GROUP), lambda i: (i, 0))
    return pl.pallas_call(
        _rope_kernel,
        grid=(t // tm,),
        in_specs=[hspec(CB_Q), hspec(CB_K), hspec(CB_V), tspec, tspec, tspec],
        out_specs=[ospec, ospec, ospec, ospec],
        out_shape=[jax.ShapeDtypeStruct((t, D_GROUP), BF16), jax.ShapeDtypeStruct((t, D_GROUP), F32),
                   jax.ShapeDtypeStruct((t, D_GROUP), BF16), jax.ShapeDtypeStruct((t, D_GROUP), BF16)],
        compiler_params=_cp(("parallel",)),
        name="rope",
    )(h, h, h, cos_t, sin1_t, sin2_t)


def _rope_tables(pos):
    half = ROPE_DIM // 2
    inv_freq = 1.0 / (ROPE_THETA ** (jnp.arange(half, dtype=F32) * 2.0 / ROPE_DIM))
    ang = pos.astype(F32)[:, None] * inv_freq
    cos, sin = jnp.cos(ang), jnp.sin(ang)
    n = pos.shape[0]
    one = jnp.ones((n, DA_DIM - ROPE_DIM), F32)
    zero = jnp.zeros((n, DA_DIM - ROPE_DIM), F32)
    z8 = jnp.zeros((n, half), F32)
    c = jnp.concatenate([cos, cos, one], 1)
    s1 = jnp.concatenate([-sin, z8, zero], 1)
    s2 = jnp.concatenate([z8, sin, zero], 1)
    return tuple(jnp.concatenate([a, a], 1) for a in (c, s1, s2))


def _ssd_kernel(z_ref, xs_ref, bc_ref, dt_ref, hist_ref, h0_ref, cw_ref, cb_ref, dtb_ref, a_ref, dsk_ref, ng_ref,
                y_ref, hist_o_ref, st_o_ref, xbuf, act, dts, ht, *, tl, q):
    hp = SSM_HEADS // SSM_GROUPS
    gw = hp * SSM_HEAD_DIM

    @pl.when(pl.program_id(1) == 0)
    def _():
        xbuf[0:SUBLANES, :] = hist_ref[0]
        ht[...] = h0_ref[0]

    xbuf[SUBLANES:SUBLANES + tl, 0:D_GROUP] = xs_ref[...]
    xbuf[SUBLANES:SUBLANES + tl, D_GROUP:2 * D_GROUP] = bc_ref[...]
    acc = cb_ref[...] + cw_ref[SSM_CONV - 1:SSM_CONV, :] * xbuf[SUBLANES:SUBLANES + tl, :]
    for k in range(SSM_CONV - 1):
        off = SUBLANES - (SSM_CONV - 1) + k
        acc = acc + cw_ref[k:k + 1, :] * xbuf[off:off + tl, :]
    act[...] = _silu(acc)
    tail = xbuf[tl:tl + SUBLANES, :]
    xbuf[0:SUBLANES, :] = tail
    hist_o_ref[0] = tail
    dts[...] = _softplus(dt_ref[...] + dtb_ref[...])

    ri = lax.broadcasted_iota(jnp.int32, (q, q), 0)
    ci = lax.broadcasted_iota(jnp.int32, (q, q), 1)
    causal = ri >= ci
    tril = causal.astype(F32)
    expand = (lax.broadcasted_iota(jnp.int32, (LANES, D_GROUP), 1) // SSM_HEAD_DIM
              == lax.broadcasted_iota(jnp.int32, (LANES, D_GROUP), 0)).astype(F32)
    eye8 = (lax.broadcasted_iota(jnp.int32, (SUBLANES, LANES), 0)
            == lax.broadcasted_iota(jnp.int32, (SUBLANES, LANES), 1)).astype(F32)
    lane_head = lax.broadcasted_iota(jnp.int32, (q, gw), 1) // SSM_HEAD_DIM

    def chunk(ci_, carry):
        r0 = pl.multiple_of(ci_ * q, q)
        dt_c = dts[pl.ds(r0, q), :]
        da = dt_c * a_ref[...]
        cs = jnp.dot(tril, da, precision=HI, preferred_element_type=F32)
        csx = jnp.dot(cs, expand, precision=HI, preferred_element_type=F32)
        dtx = jnp.dot(dt_c, expand, precision=HI, preferred_element_type=F32)
        cst = lax.dot_general(eye8, cs, NT, precision=HI, preferred_element_type=F32)
        dtt = lax.dot_general(eye8, dt_c, NT, precision=HI, preferred_element_type=F32)
        xs = act[pl.ds(r0, q), 0:D_GROUP]
        ecs = jnp.exp(csx)
        cs_last = csx[q - 1:q, :]
        xw = (xs * (jnp.exp(cs_last - csx) * dtx)).astype(BF16)
        xsb = xs.astype(BF16)
        dec = jnp.exp(cs_last)
        ys = []
        for g in range(SSM_GROUPS):
            gl = slice(g * gw, (g + 1) * gw)
            bg = act[pl.ds(r0, q), D_GROUP + g * SSM_STATE:D_GROUP + (g + 1) * SSM_STATE].astype(BF16)
            c0 = D_GROUP + SSM_GROUPS * SSM_STATE
            cg = act[pl.ds(r0, q), c0 + g * SSM_STATE:c0 + (g + 1) * SSM_STATE].astype(BF16)
            cb = lax.dot_general(cg, bg, NT, preferred_element_type=F32)
            hin = ht[:, gl]
            yg = jnp.dot(cg, hin.astype(BF16), preferred_element_type=F32) * ecs[:, gl]
            xg = xsb[:, gl]
            for hh in range(hp):
                h = g * hp + hh
                seg = cs[:, h:h + 1] - cst[h:h + 1, :]
                lm = jnp.exp(jnp.where(causal, seg, -jnp.inf))
                m = (cb * lm * dtt[h:h + 1, :]).astype(BF16)
                xm = jnp.where(lane_head == hh, xg, jnp.zeros_like(xg))
                yg = yg + jnp.dot(m, xm, preferred_element_type=F32)
            st = lax.dot_general(bg, xw[:, gl], TN, preferred_element_type=F32)
            ht[:, gl] = dec[:, gl] * hin + st
            ys.append(yg)
        y = jnp.concatenate(ys, axis=1) + dsk_ref[...] * xs
        y = y * _silu(z_ref[pl.ds(r0, q), :])
        outs = []
        for g in range(SSM_GROUPS):
            yg = y[:, g * gw:(g + 1) * gw]
            outs.append(yg * lax.rsqrt(jnp.mean(yg * yg, -1, keepdims=True) + EPS))
        y = jnp.concatenate(outs, axis=1) * ng_ref[...]
        y_ref[pl.ds(r0, q), :] = y.astype(y_ref.dtype)
        return carry

    lax.fori_loop(0, tl // q, chunk, 0)
    st_o_ref[0] = ht[...]


def _ssd(h, dt, hist, h0t, p, *, nb, tl, q, row0, out_rows, orow0=0):
    rb0 = row0 // tl
    ob0 = orow0 // tl
    nt = (p["len"]) // tl

    def hspec(cb):
        return pl.BlockSpec((tl, D_GROUP), lambda b, t, cb=cb: (rb0 + b * nt + t, cb))

    full = lambda shape: pl.BlockSpec(shape, lambda b, t: (0,) * len(shape))
    kern = functools.partial(_ssd_kernel, tl=tl, q=q)
    return pl.pallas_call(
        kern,
        grid=(nb, nt),
        in_specs=[hspec(CB_Z), hspec(CB_XS), hspec(CB_BC),
                  pl.BlockSpec((tl, LANES), lambda b, t: (rb0 + b * nt + t, 0)),
                  pl.BlockSpec((1, SUBLANES, SSM_CONV_CH), lambda b, t: (b, 0, 0)),
                  pl.BlockSpec((1, SSM_STATE, D_GROUP), lambda b, t: (b, 0, 0)),
                  full((SSM_CONV, SSM_CONV_CH)), full((1, SSM_CONV_CH)), full((1, LANES)), full((1, LANES)),
                  full((1, D_GROUP)), full((1, D_GROUP))],
        out_specs=[pl.BlockSpec((tl, D_GROUP), lambda b, t: (ob0 + b * nt + t, 0)),
                   pl.BlockSpec((1, SUBLANES, SSM_CONV_CH), lambda b, t: (b, 0, 0)),
                   pl.BlockSpec((1, SSM_STATE, D_GROUP), lambda b, t: (b, 0, 0))],
        out_shape=[jax.ShapeDtypeStruct((out_rows, D_GROUP), BF16),
                   jax.ShapeDtypeStruct((nb, SUBLANES, SSM_CONV_CH), F32),
                   jax.ShapeDtypeStruct((nb, SSM_STATE, D_GROUP), F32)],
        scratch_shapes=[pltpu.VMEM((tl + SUBLANES, SSM_CONV_CH), F32), pltpu.VMEM((tl, SSM_CONV_CH), F32),
                        pltpu.VMEM((tl, LANES), F32), pltpu.VMEM((SSM_STATE, D_GROUP), F32)],
        compiler_params=_cp(("parallel", "arbitrary")),
        name=p["name"],
    )(h, h, h, dt, hist, h0t, p["cw"], p["cb"], p["dtb"], p["a"], p["dsk"], p["ng"])


def _attn_kernel(lam_ref, q_ref, k_ref, v_ref, g_ref, o_ref, m_ref, l_ref, acc_ref, *, tq, scale_out):
    i = pl.program_id(1)
    j = pl.program_id(2)

    @pl.when(j == 0)
    def _():
        m_ref[...] = jnp.full(m_ref.shape, -jnp.inf, F32)
        l_ref[...] = jnp.zeros(l_ref.shape, F32)
        acc_ref[...] = jnp.zeros(acc_ref.shape, F32)

    @pl.when(j <= i)
    def _():
        q = q_ref[...]
        k = k_ref[...]
        v = v_ref[...]
        row = i * tq + lax.broadcasted_iota(jnp.int32, (tq, tq), 0)
        col = j * tq + lax.broadcasted_iota(jnp.int32, (tq, tq), 1)
        mask = (col // CHUNK) <= (row // CHUNK)
        for c in range(2):
            sl = slice(c * DA_DIM, (c + 1) * DA_DIM)
            s = lax.dot_general(q[:, sl], k[:, sl], NT, preferred_element_type=F32)
            s = jnp.where(mask, s, -jnp.inf)
            m_prev = m_ref[c]
            m_new = jnp.maximum(m_prev, jnp.max(s, -1, keepdims=True))
            alpha = jnp.exp(m_prev - m_new)
            p = jnp.exp(s - m_new)
            l_ref[c] = alpha * l_ref[c] + jnp.sum(p, -1, keepdims=True)
            acc_ref[c] = alpha * acc_ref[c] + jnp.dot(p.astype(BF16), v, preferred_element_type=F32)
            m_ref[c] = m_new

    @pl.when(j == i)
    def _():
        lam = lam_ref[0]
        o = acc_ref[0] / l_ref[0] - lam * (acc_ref[1] / l_ref[1])
        ms = jnp.mean(o * o, -1, keepdims=True)
        o_ref[...] = (o * lax.rsqrt(ms + EPS) * g_ref[...] * scale_out).astype(o_ref.dtype)


def _attn_prompt(lam, qb, kb, vb, g, *, tp, out_rows, scale_out):
    tq = _pick(tp, (512, 256, 128))
    n = tp // tq
    kern = functools.partial(_attn_kernel, tq=tq, scale_out=scale_out)
    return pl.pallas_call(
        kern,
        grid=(DA_HEADS, n, n),
        in_specs=[pl.BlockSpec(memory_space=pltpu.SMEM),
                  pl.BlockSpec((tq, LANES), lambda h, i, j: (i, h)),
                  pl.BlockSpec((tq, LANES), lambda h, i, j: (jnp.minimum(j, i), h)),
                  pl.BlockSpec((tq, LANES), lambda h, i, j: (jnp.minimum(j, i), h)),
                  pl.BlockSpec((1, LANES), lambda h, i, j: (0, 0))],
        out_specs=pl.BlockSpec((tq, LANES), lambda h, i, j: (i, h)),
        out_shape=jax.ShapeDtypeStruct((out_rows, D_GROUP), BF16),
        scratch_shapes=[pltpu.VMEM((2, tq, 1), F32), pltpu.VMEM((2, tq, 1), F32), pltpu.VMEM((2, tq, LANES), F32)],
        compiler_params=_cp(("parallel", "parallel", "arbitrary")),
        name="attn_prompt",
    )(lam, qb, kb, vb, g)


def _attn_s_kernel(lam_ref, q_ref, kc_ref, vc_ref, kn_ref, vn_ref, g_ref, o_ref, *, scale_out):
    q = q_ref[...]
    kc = kc_ref[0].astype(BF16)
    vc = vc_ref[0].astype(BF16)
    kn = kn_ref[...]
    vn = vn_ref[...]
    outs = []
    for c in range(2):
        sl = slice(c * DA_DIM, (c + 1) * DA_DIM)
        sp = lax.dot_general(q[:, sl], kc[:, sl], NT, preferred_element_type=F32)
        sn = lax.dot_general(q[:, sl], kn[:, sl], NT, preferred_element_type=F32)
        m = jnp.maximum(jnp.max(sp, -1, keepdims=True), jnp.max(sn, -1, keepdims=True))
        pp = jnp.exp(sp - m)
        pn = jnp.exp(sn - m)
        l = jnp.sum(pp, -1, keepdims=True) + jnp.sum(pn, -1, keepdims=True)
        o = jnp.dot(pp.astype(BF16), vc, preferred_element_type=F32) + jnp.dot(pn.astype(BF16), vn,
                                                                               preferred_element_type=F32)
        outs.append(o / l)
    o = outs[0] - lam_ref[0] * outs[1]
    ms = jnp.mean(o * o, -1, keepdims=True)
    o_ref[...] = (o * lax.rsqrt(ms + EPS) * g_ref[...] * scale_out).astype(o_ref.dtype)


def _attn_sample(lam, qb, kb, vb, cache_k, cache_v, g, *, layer, nb, ls, row0, scale_out):
    past = cache_k.shape[1]
    rb0 = row0 // ls
    new = lambda: pl.BlockSpec((ls, LANES), lambda b, h: (rb0 + b, h))
    cache = lambda: pl.BlockSpec((1, past, LANES), lambda b, h: (layer * nb + b, 0, h))
    kern = functools.partial(_attn_s_kernel, scale_out=scale_out)
    return pl.pallas_call(
        kern,
        grid=(nb, DA_HEADS),
        in_specs=[pl.BlockSpec(memory_space=pltpu.SMEM),
                  new(), cache(), cache(), new(), new(), pl.BlockSpec((1, LANES), lambda b, h: (0, 0))],
        out_specs=pl.BlockSpec((ls, LANES), lambda b, h: (b, h)),
        out_shape=jax.ShapeDtypeStruct((nb * ls, D_GROUP), BF16),
        compiler_params=_cp(("parallel", "parallel")),
        name="attn_sample",
    )(lam, qb, cache_k, cache_v, kb, vb, g)


def _sgu_kernel(u_ref, v_ref, g_ref, b_ref, w_ref, bias_ref, y_ref, vo_ref, *, tl, q):
    v = _ln(v_ref[...], g_ref[...], b_ref[...])
    if vo_ref is not None:
        vo_ref[...] = v
    vb = v.astype(BF16)
    tri = lax.broadcasted_iota(jnp.int32, (q, q), 0) >= lax.broadcasted_iota(jnp.int32, (q, q), 1)
    for g in range(SG_GROUPS):
        w = jnp.where(tri, w_ref[g], 0.0).astype(BF16)
        sl = slice(g * LANES, (g + 1) * LANES)
        for c in range(tl // q):
            rs = slice(c * q, (c + 1) * q)
            s = jnp.dot(w, vb[rs, sl], preferred_element_type=F32) + bias_ref[:, sl]
            y_ref[rs, sl] = (u_ref[rs, sl] * s).astype(y_ref.dtype)


def _sgu(h, p, *, nrows, tl, q, row0, out_rows, want_v, orow0=0):
    rb0 = row0 // tl
    ob0 = orow0 // tl
    hspec = lambda cb: pl.BlockSpec((tl, D_GROUP), lambda i, cb=cb: (rb0 + i, cb))
    full = lambda shape: pl.BlockSpec(shape, lambda i: (0,) * len(shape))
    out_specs = [pl.BlockSpec((tl, D_GROUP), lambda i: (ob0 + i, 0))]
    out_shape = [jax.ShapeDtypeStruct((out_rows, D_GROUP), BF16)]
    if want_v:
        out_specs.append(pl.BlockSpec((tl, D_GROUP), lambda i: (i, 0)))
        out_shape.append(jax.ShapeDtypeStruct((nrows, D_GROUP), F32))
        kern = functools.partial(_sgu_kernel, tl=tl, q=q)
    else:
        kern = lambda u, v, g, b, w, bias, y: _sgu_kernel(u, v, g, b, w, bias, y, None, tl=tl, q=q)
    return pl.pallas_call(
        kern,
        grid=(nrows // tl,),
        in_specs=[hspec(CB_SU), hspec(CB_SV), full((1, D_GROUP)), full((1, D_GROUP)),
                  full((SG_GROUPS, q, q)), full((q, D_GROUP))],
        out_specs=out_specs,
        out_shape=out_shape,
        compiler_params=_cp(("parallel",)),
        name=p["name"],
    )(h, h, p["g"], p["b"], p["w"], p["bias"])


CV_HIST = 32
CV_RB = 64


def _cconv_kernel(a_ref, gt_ref, hist_ref, w_ref, b_ref, lg_ref, lb_ref, y_ref, tail_ref, buf, *, tl):
    @pl.when(pl.program_id(1) == 0)
    def _():
        buf[0:CV_HIST, :] = hist_ref[0]

    buf[CV_HIST:CV_HIST + tl, :] = a_ref[...] * jax.nn.sigmoid(gt_ref[...])
    rb = min(CV_RB, tl)
    pad = CV_HIST - (CV_WIDTH - 1)
    for r0 in range(0, tl, rb):
        acc = jnp.broadcast_to(b_ref[...], (rb, D_GROUP))
        for k in range(CV_WIDTH):
            acc = acc + w_ref[k:k + 1, :] * buf[r0 + k + pad:r0 + k + pad + rb, :]
        y_ref[r0:r0 + rb, :] = _silu(_ln(acc, lg_ref[...], lb_ref[...])).astype(y_ref.dtype)
    tail = buf[tl:tl + CV_HIST, :]
    buf[0:CV_HIST, :] = tail
    tail_ref[0] = tail


def _cconv(h, hist, p, *, nb, seqlen, tl, row0, out_rows, orow0=0):
    rb0 = row0 // tl
    ob0 = orow0 // tl
    nt = seqlen // tl
    hspec = lambda cb: pl.BlockSpec((tl, D_GROUP), lambda b, t, cb=cb: (rb0 + b * nt + t, cb))
    full = lambda shape: pl.BlockSpec(shape, lambda b, t: (0,) * len(shape))
    kern = functools.partial(_cconv_kernel, tl=tl)
    return pl.pallas_call(
        kern,
        grid=(nb, nt),
        in_specs=[hspec(CB_CA), hspec(CB_CG), pl.BlockSpec((1, CV_HIST, D_GROUP), lambda b, t: (b, 0, 0)),
                  full((CV_HIST, D_GROUP)), full((1, D_GROUP)), full((1, D_GROUP)), full((1, D_GROUP))],
        out_specs=[pl.BlockSpec((tl, D_GROUP), lambda b, t: (ob0 + b * nt + t, 0)),
                   pl.BlockSpec((1, CV_HIST, D_GROUP), lambda b, t: (b, 0, 0))],
        out_shape=[jax.ShapeDtypeStruct((out_rows, D_GROUP), BF16),
                   jax.ShapeDtypeStruct((nb, CV_HIST, D_GROUP), F32)],
        scratch_shapes=[pltpu.VMEM((tl + CV_HIST, D_GROUP), F32)],
        compiler_params=_cp(("parallel", "arbitrary")),
        name=p["name"],
    )(h, h, hist, p["w"], p["b"], p["lg"], p["lb"])


R_ROWS = 32


def _outproj_kernel(ya_ref, yb_ref, yc_ref, yd_ref, x_ref, w_ref, g_ref, b_ref, wr_ref, br_ref,
                    x1_ref, gate_ref, eid_ref):
    acc = jnp.dot(ya_ref[...], w_ref[0], preferred_element_type=F32)
    acc = acc + jnp.dot(yb_ref[...], w_ref[1], preferred_element_type=F32)
    acc = acc + jnp.dot(yc_ref[...], w_ref[2], preferred_element_type=F32)
    acc = acc + jnp.dot(yd_ref[...], w_ref[3], preferred_element_type=F32)
    x1 = _ln(ALPHA * x_ref[...] + acc, g_ref[...], b_ref[...])
    x1_ref[...] = x1
    tm = x1.shape[0]
    lt = lax.dot_general(wr_ref[...], x1, NT, precision=HI, preferred_element_type=F32) + br_ref[...]
    lg = lt[0:N_EGROUPS, :]
    ridx = lax.broadcasted_iota(jnp.int32, (N_EGROUPS, tm), 0).astype(F32)

    def softmax0(a):
        e = jnp.exp(a - jnp.max(a, 0, keepdims=True))
        return e / jnp.sum(e, 0, keepdims=True)

    def top1(pv):
        best = jnp.max(pv, 0, keepdims=True)
        return best, jnp.min(jnp.where(pv == best, ridx, float(N_EGROUPS)), 0, keepdims=True)

    p_g, g_idx = top1(softmax0(lg))
    le = jnp.zeros((EXP_PER_GROUP, tm), F32)
    for g in range(N_EGROUPS):
        r0 = SUBLANES + g * EXP_PER_GROUP
        le = le + jnp.where(g_idx == g, lt[r0:r0 + EXP_PER_GROUP, :], 0.0)
    pe = softmax0(le)
    p1, i1 = top1(pe)
    p2, i2 = top1(jnp.where(ridx == i1, -1.0, pe))
    den = p1 + p2
    gate_ref[...] = jnp.zeros(gate_ref.shape, F32)
    eid_ref[...] = jnp.zeros(eid_ref.shape, jnp.int32)
    gate_ref[0:1, :] = p_g * p1 / den
    gate_ref[1:2, :] = p_g * p2 / den
    eid_ref[0:1, :] = (g_idx * EXP_PER_GROUP + i1).astype(jnp.int32)
    eid_ref[1:2, :] = (g_idx * EXP_PER_GROUP + i2).astype(jnp.int32)


def _outproj(ys, x, w_out, g, b, wr, br):
    t = x.shape[0]
    tm = _pick(t, (640, 128))
    yspec = pl.BlockSpec((tm, D_GROUP), lambda i: (i, 0))
    full = lambda shape: pl.BlockSpec(shape, lambda i: (0,) * len(shape))
    return pl.pallas_call(
        _outproj_kernel,
        grid=(t // tm,),
        in_specs=[yspec, yspec, yspec, yspec, pl.BlockSpec((tm, D_MODEL), lambda i: (i, 0)),
                  full((4, D_GROUP, D_MODEL)), full((1, D_MODEL)), full((1, D_MODEL)),
                  full((R_ROWS, D_MODEL)), full((R_ROWS, 1))],
        out_specs=[pl.BlockSpec((tm, D_MODEL), lambda i: (i, 0)),
                   pl.BlockSpec((SUBLANES, tm), lambda i: (0, i)),
                   pl.BlockSpec((SUBLANES, tm), lambda i: (0, i))],
        out_shape=[jax.ShapeDtypeStruct((t, D_MODEL), F32), jax.ShapeDtypeStruct((SUBLANES, t), F32),
                   jax.ShapeDtypeStruct((SUBLANES, t), jnp.int32)],
        compiler_params=_cp(("parallel",)),
        name="outproj_ln_router",
    )(*ys, x, w_out, g, b, wr, br)


def _moe_row_copy(x_hbm, xbuf, sem, tok, slot, r):
    return pltpu.make_async_copy(x_hbm.at[pl.ds(tok, 1), :], xbuf.at[slot, pl.ds(r, 1), :], sem.at[slot])


def _moe_kernel(te_ref, nu_ref, src_ref, x_hbm, gate_ref, w1_ref, w3_ref, w2_ref, y_ref, xbuf, sem, *, tm):
    i = pl.program_id(0)
    n_used = nu_ref[0]

    def start_tile(tile, slot):
        def body(r, c):
            _moe_row_copy(x_hbm, xbuf, sem, src_ref[tile * tm + r], slot, r).start()
            return c
        lax.fori_loop(0, tm, body, 0)

    @pl.when(i == 0)
    def _():
        start_tile(0, 0)

    @pl.when(i + 1 < n_used)
    def _():
        start_tile(i + 1, (i + 1) % 2)

    @pl.when(i < n_used)
    def _():
        slot = i % 2

        def wbody(r, c):
            _moe_row_copy(x_hbm, xbuf, sem, 0, slot, r).wait()
            return c
        lax.fori_loop(0, tm, wbody, 0)
        x = xbuf[slot].astype(BF16)
        h1 = jnp.dot(x, w1_ref[0], preferred_element_type=F32)
        h3 = jnp.dot(x, w3_ref[0], preferred_element_type=F32)
        hh = (_silu(h1) * h3).astype(BF16)
        y_ref[...] = jnp.dot(hh, w2_ref[0], preferred_element_type=F32) * gate_ref[...]

    @pl.when(i >= n_used)
    def _():
        y_ref[...] = jnp.zeros(y_ref.shape, F32)


def _moe(tile_expert, n_used, src_tok, x1, gate_sorted, w1, w3, w2, *, n_tiles):
    tm = MOE_TM
    kern = functools.partial(_moe_kernel, tm=tm)
    grid_spec = pltpu.PrefetchScalarGridSpec(
        num_scalar_prefetch=3,
        grid=(n_tiles,),
        in_specs=[pl.BlockSpec(memory_space=pl.ANY),
                  pl.BlockSpec((tm, 1), lambda i, te, nu, src: (i, 0)),
                  pl.BlockSpec((1, D_MODEL, D_EXPERT), lambda i, te, nu, src: (te[i], 0, 0)),
                  pl.BlockSpec((1, D_MODEL, D_EXPERT), lambda i, te, nu, src: (te[i], 0, 0)),
                  pl.BlockSpec((1, D_EXPERT, D_MODEL), lambda i, te, nu, src: (te[i], 0, 0))],
        out_specs=pl.BlockSpec((tm, D_MODEL), lambda i, te, nu, src: (i, 0)),
        scratch_shapes=[pltpu.VMEM((2, tm, D_MODEL), F32), pltpu.SemaphoreType.DMA((2,))],
    )
    return pl.pallas_call(
        kern,
        grid_spec=grid_spec,
        out_shape=jax.ShapeDtypeStruct((n_tiles * tm, D_MODEL), F32),
        compiler_params=_cp(("arbitrary",)),
        name="moe_grouped",
    )(tile_expert, n_used, src_tok, x1, gate_sorted, w1, w3, w2)


def _moe_plan(eid, gates, t, n_tiles):
    tm = MOE_TM
    e_flat = eid[:2].reshape(-1)
    g_flat = gates[:2].reshape(-1)
    npair = 2 * t
    pair_id = jnp.arange(npair, dtype=jnp.int32)
    e_sorted, order = lax.sort((e_flat, pair_id), num_keys=1, is_stable=True)
    counts = jnp.sum((e_flat[:, None] == jnp.arange(N_EXPERTS, dtype=jnp.int32)[None, :]).astype(jnp.int32), 0)
    ntile_e = (counts + tm - 1) // tm
    tile_end = jnp.cumsum(ntile_e)
    tile_start = tile_end - ntile_e
    unp_start = jnp.cumsum(counts) - counts
    shift = tile_start * tm - unp_start
    n_used = tile_end[-1]
    tiles = jnp.arange(n_tiles, dtype=jnp.int32)
    te = jnp.sum((tiles[:, None] >= tile_end[None, :]).astype(jnp.int32), 1)
    last_e = jnp.sum((jnp.maximum(n_used - 1, 0) >= tile_end).astype(jnp.int32))
    te = jnp.where(tiles < n_used, te, last_e).astype(jnp.int32)
    rows = jnp.arange(n_tiles * tm, dtype=jnp.int32)
    e_r = te[rows // tm]
    sidx = rows - shift[e_r]
    valid = (rows // tm < n_used) & (sidx < unp_start[e_r] + counts[e_r])
    sidx = jnp.clip(sidx, 0, npair - 1)
    pair_r = order[sidx]
    src_tok = jnp.where(valid, pair_r % t, 0).astype(jnp.int32)
    gate_sorted = jnp.where(valid, g_flat[pair_r], 0.0).reshape(-1, 1)
    pos = jnp.zeros((npair,), jnp.int32).at[order].set(pair_id + shift[e_sorted], unique_indices=True)
    return te, n_used.reshape(1).astype(jnp.int32), src_tok, gate_sorted, pos


def _comb_row_copy(y_hbm, ybuf, sem, row, slot, k, r):
    return pltpu.make_async_copy(y_hbm.at[pl.ds(row, 1), :], ybuf.at[slot, k, pl.ds(r, 1), :], sem.at[slot])


def _combine_kernel(pos_ref, y_hbm, x1_ref, g_ref, b_ref, xf_ref, xb_ref, ybuf, sem, *, tm, t):
    i = pl.program_id(0)
    n = pl.num_programs(0)

    def start_tile(tile, slot):
        def body(r, c):
            for k in range(2):
                _comb_row_copy(y_hbm, ybuf, sem, pos_ref[k * t + tile * tm + r], slot, k, r).start()
            return c
        lax.fori_loop(0, tm, body, 0)

    @pl.when(i == 0)
    def _():
        start_tile(0, 0)

    @pl.when(i + 1 < n)
    def _():
        start_tile(i + 1, (i + 1) % 2)

    slot = i % 2

    def wbody(r, c):
        for k in range(2):
            _comb_row_copy(y_hbm, ybuf, sem, 0, slot, k, r).wait()
        return c
    lax.fori_loop(0, tm, wbody, 0)
    x2 = _ln(ALPHA * x1_ref[...] + (ybuf[slot, 0] + ybuf[slot, 1]), g_ref[...], b_ref[...])
    xf_ref[...] = x2
    xb_ref[...] = x2.astype(BF16)


def _combine(pos, y_sorted, x1, g, b):
    t = x1.shape[0]
    tm = _pick(t, (320, 128))
    kern = functools.partial(_combine_kernel, tm=tm, t=t)
    grid_spec = pltpu.PrefetchScalarGridSpec(
        num_scalar_prefetch=1,
        grid=(t // tm,),
        in_specs=[pl.BlockSpec(memory_space=pl.ANY),
                  pl.BlockSpec((tm, D_MODEL), lambda i, pos: (i, 0)),
                  pl.BlockSpec((1, D_MODEL), lambda i, pos: (0, 0)),
                  pl.BlockSpec((1, D_MODEL), lambda i, pos: (0, 0))],
        out_specs=[pl.BlockSpec((tm, D_MODEL), lambda i, pos: (i, 0)),
                   pl.BlockSpec((tm, D_MODEL), lambda i, pos: (i, 0))],
        scratch_shapes=[pltpu.VMEM((2, 2, tm, D_MODEL), F32), pltpu.SemaphoreType.DMA((2,))],
    )
    return pl.pallas_call(
        kern,
        grid_spec=grid_spec,
        out_shape=[jax.ShapeDtypeStruct((t, D_MODEL), F32), jax.ShapeDtypeStruct((t, D_MODEL), BF16)],
        compiler_params=_cp(("arbitrary",)),
        name="moe_combine_ln",
    )(pos, y_sorted, x1, g, b)


def _row(v, n=None):
    v = v.reshape(1, -1).astype(F32)
    if n is not None and v.shape[1] < n:
        v = jnp.pad(v, ((0, 0), (0, n - v.shape[1])))
    return v


def kernel(x_prompt, x_sample, state_ssm_conv, state_ssm, cache_k, cache_v, state_conv, ln_in_g, ln_in_b, w_in, ssm_conv_w, ssm_conv_b, ssm_dt_bias, ssm_a_log, ssm_d, ssm_norm_g, da_lq1, da_lk1, da_lq2, da_lk2, da_norm_g, sg_ln_g, sg_ln_b, sg_w, sg_b, cv_w, cv_b, cv_ln_g, cv_ln_b, w_out, ln1_g, ln1_b, moe_wg_group, moe_bg_group, moe_wg_exp, moe_bg_exp, moe_w1, moe_w3, moe_w2, ln2_g, ln2_b):
    bp, tp, _ = x_prompt.shape
    nb, ls, _ = x_sample.shape
    depth = w_in.shape[0]
    past = cache_k.shape[2]
    assert bp == 1 and tp % 512 == 0 and ls % SUBLANES == 0 and ls <= CHUNK
    ts = nb * ls
    t = tp + ts
    n_tiles = (2 * t + N_EXPERTS * (MOE_TM - 1)) // MOE_TM + 1

    dt0 = D_GROUP + SSM_CONV_CH
    w_main = jnp.concatenate([w_in[:, :, :dt0], w_in[:, :, dt0 + SSM_HEADS:]], -1).astype(BF16)
    w_dt = jnp.pad(w_in[:, :, dt0:dt0 + SSM_HEADS], ((0, 0), (0, 0), (0, LANES - SSM_HEADS))).astype(BF16)
    w_out_b = w_out.reshape(depth, 4, D_GROUP, D_MODEL).astype(BF16)
    w1b, w3b, w2b = moe_w1.astype(BF16), moe_w3.astype(BF16), moe_w2.astype(BF16)
    wr = jnp.zeros((depth, R_ROWS, D_MODEL), F32)
    wr = wr.at[:, 0:N_EGROUPS].set(jnp.swapaxes(moe_wg_group, 1, 2))
    wr = wr.at[:, SUBLANES:SUBLANES + N_EXPERTS].set(jnp.swapaxes(moe_wg_exp, 1, 2))
    br = jnp.zeros((depth, R_ROWS, 1), F32)
    br = br.at[:, 0:N_EGROUPS, 0].set(moe_bg_group).at[:, SUBLANES:SUBLANES + N_EXPERTS, 0].set(moe_bg_exp)
    cache_k2 = cache_k.reshape(depth * nb, past, D_GROUP)
    cache_v2 = cache_v.reshape(depth * nb, past, D_GROUP)

    pos_all = jnp.concatenate([jnp.arange(tp), jnp.tile(past + jnp.arange(ls), nb)])
    cos_t, sin1_t, sin2_t = _rope_tables(pos_all)

    x_all = jnp.concatenate([x_prompt.reshape(tp, D_MODEL), x_sample.reshape(ts, D_MODEL)], 0)
    xf, xb = _ln_in(x_all, ln_in_g, ln_in_b)

    zeros_hist = jnp.zeros((1, SUBLANES, SSM_CONV_CH), F32)
    zeros_state = jnp.zeros((1, SSM_STATE, D_GROUP), F32)
    zeros_cv = jnp.zeros((1, CV_HIST, D_GROUP), F32)
    outs = {k: [] for k in ("p_sc", "p_ss", "p_k", "p_v", "p_cv", "s_sc", "s_ss", "s_k", "s_v", "s_cv", "s_sg")}
    tl_p = 512
    tl_c = 256

    for l in range(depth):
        lam_init = 0.8 - 0.6 * math.exp(-0.3 * l)
        h, dt = _inproj(xb, w_main[l], w_dt[l])

        ssd_p = dict(cw=ssm_conv_w[l], cb=_row(ssm_conv_b[l]), dtb=_row(ssm_dt_bias[l], LANES),
                     a=_row(-jnp.exp(ssm_a_log[l].astype(F32)), LANES),
                     dsk=_row(jnp.repeat(ssm_d[l], SSM_HEAD_DIM)), ng=_row(ssm_norm_g[l]))
        ya, hist_p, st_p = _ssd(h, dt, zeros_hist, zeros_state, dict(ssd_p, len=tp, name="ssd_prompt"),
                                nb=1, tl=tl_p, q=CHUNK, row0=0, out_rows=tp)
        hist_s_in = jnp.pad(state_ssm_conv[l], ((0, 0), (SUBLANES - (SSM_CONV - 1), 0), (0, 0)))
        h0t_s = jnp.swapaxes(state_ssm[l].reshape(nb, D_GROUP, SSM_STATE), 1, 2)
        ya_s, hist_s, st_s = _ssd(h, dt, hist_s_in, h0t_s, dict(ssd_p, len=ls, name="ssd_sample"),
                                  nb=nb, tl=ls, q=ls, row0=tp, out_rows=ts)
        ya = jnp.concatenate([ya, ya_s], 0)
        outs["p_sc"].append(hist_p[:, SUBLANES - (SSM_CONV - 1):])
        outs["s_sc"].append(hist_s[:, SUBLANES - (SSM_CONV - 1):])
        unt = lambda s: jnp.swapaxes(s, 1, 2).reshape(-1, SSM_HEADS, SSM_HEAD_DIM, SSM_STATE)
        outs["p_ss"].append(unt(st_p))
        outs["s_ss"].append(unt(st_s))

        qb, kf, kb, vb = _rope(h, cos_t, sin1_t, sin2_t)
        f32 = lambda a: a.astype(F32)
        lam = (jnp.exp(jnp.sum(f32(da_lq1[l]) * f32(da_lk1[l]))) - jnp.exp(jnp.sum(f32(da_lq2[l]) * f32(da_lk2[l])))
               + lam_init).reshape(1).astype(F32)
        gda = _row(da_norm_g[l])
        yb = _attn_prompt(lam, qb, kb, vb, gda, tp=tp, out_rows=tp, scale_out=1.0 - lam_init)
        yb_s = _attn_sample(lam, qb, kb, vb, cache_k2, cache_v2, gda, layer=l, nb=nb, ls=ls, row0=tp,
                            scale_out=1.0 - lam_init)
        yb = jnp.concatenate([yb, yb_s], 0)
        outs["p_k"].append(kf[:tp].reshape(1, tp, DA_HEADS, 2, DA_DIM))
        outs["s_k"].append(kf[tp:].reshape(nb, ls, DA_HEADS, 2, DA_DIM))
        vf = h[:, CB_V * D_GROUP:(CB_V + 1) * D_GROUP]
        outs["p_v"].append(vf[:tp].reshape(1, tp, DA_HEADS, DA_VDIM))
        outs["s_v"].append(vf[tp:].reshape(nb, ls, DA_HEADS, DA_VDIM))

        sg_common = dict(g=_row(sg_ln_g[l]), b=_row(sg_ln_b[l]))
        bias_full = lambda qq: jnp.repeat(sg_b[l][:, :qq].T, LANES, axis=1).astype(F32)
        (yc,) = _sgu(h, dict(sg_common, w=sg_w[l], bias=bias_full(SG_CHUNK), name="sgu_prompt"),
                     nrows=tp, tl=tl_p, q=SG_CHUNK, row0=0, out_rows=tp, want_v=False)
        yc_s, v_rows = _sgu(h, dict(sg_common, w=sg_w[l][:, :ls, :ls], bias=bias_full(ls), name="sgu_sample"),
                            nrows=ts, tl=ls, q=ls, row0=tp, out_rows=ts, want_v=True)
        yc = jnp.concatenate([yc, yc_s], 0)
        outs["s_sg"].append(v_rows.reshape(nb, ls, D_GROUP))

        cv_p = dict(w=jnp.pad(cv_w[l], ((0, CV_HIST - CV_WIDTH), (0, 0))), b=_row(cv_b[l]),
                    lg=_row(cv_ln_g[l]), lb=_row(cv_ln_b[l]))
        yd, tail_p = _cconv(h, zeros_cv, dict(cv_p, name="cconv_prompt"), nb=1, seqlen=tp, tl=tl_c, row0=0,
                            out_rows=tp)
        cv_hist_s = jnp.pad(state_conv[l], ((0, 0), (CV_HIST - (CV_WIDTH - 1), 0), (0, 0)))
        yd_s, tail_s = _cconv(h, cv_hist_s, dict(cv_p, name="cconv_sample"), nb=nb, seqlen=ls, tl=ls, row0=tp,
                              out_rows=ts)
        yd = jnp.concatenate([yd, yd_s], 0)
        outs["p_cv"].append(tail_p[:, CV_HIST - (CV_WIDTH - 1):])
        outs["s_cv"].append(tail_s[:, CV_HIST - (CV_WIDTH - 1):])

        x1, gates, eid = _outproj((ya, yb, yc, yd), xf, w_out_b[l], _row(ln1_g[l]), _row(ln1_b[l]), wr[l], br[l])
        te, n_used, src_tok, gate_sorted, pos = _moe_plan(eid, gates, t, n_tiles)
        y_sorted = _moe(te, n_used, src_tok, x1, gate_sorted, w1b[l], w3b[l], w2b[l], n_tiles=n_tiles)
        xf, xb = _combine(pos, y_sorted, x1, _row(ln2_g[l]), _row(ln2_b[l]))

    st = lambda k: jnp.stack(outs[k])
    return (xf[:tp].reshape(1, tp, D_MODEL), xf[tp:].reshape(nb, ls, D_MODEL),
            st("p_sc"), st("p_ss"), st("p_k"), st("p_v"), st("p_cv"),
            st("s_sc"), st("s_ss"), st("s_k"), st("s_v"), st("s_cv"), st("s_sg"))
```

```python
import functools
import math

import jax
import jax.numpy as jnp
from jax import lax
from jax.experimental import pallas as pl
from jax.experimental.pallas import tpu as pltpu

F32 = jnp.float32
BF16 = jnp.bfloat16
HI = lax.Precision.HIGHEST

D_MODEL = 2048
DEPTH = 4
CHUNK = 64
D_GROUP = 512
SSM_HEADS = 8
SSM_HEAD_DIM = 64
SSM_GROUPS = 2
SSM_STATE = 128
SSM_CONV = 4
SSM_CONV_CH = 1024
DA_HEADS = 4
DA_DIM = 64
DA_VDIM = 128
ROPE_DIM = 16
ROPE_THETA = 500000.0
SG_CHUNK = 128
SG_GROUPS = 4
CV_WIDTH = 31
N_EGROUPS = 4
EXP_PER_GROUP = 4
N_EXPERTS = 16
D_EXPERT = 512
ALPHA = (2 * DEPTH) ** 0.25
EPS = 1e-5

LANES = 128
SUBLANES = 8
VMEM_LIMIT = 56 * 1024 * 1024
D_MAIN = 10 * D_GROUP
MOE_TM = 256
NT = (((1,), (1,)), ((), ()))
TN = (((0,), (0,)), ((), ()))

CB_Z, CB_XS, CB_BC, CB_Q, CB_K, CB_V, CB_SU, CB_SV, CB_CA, CB_CG = range(10)


def _cp(sem):
    return pltpu.CompilerParams(dimension_semantics=sem, vmem_limit_bytes=VMEM_LIMIT)


def _pick(n, cands):
    for c in cands:
        if n % c == 0:
            return c
    raise ValueError(f"no tile for {n}")


def _ln(x, g, b):
    mu = jnp.mean(x, -1, keepdims=True)
    xc = x - mu
    var = jnp.mean(xc * xc, -1, keepdims=True)
    return xc * lax.rsqrt(var + EPS) * g + b


def _silu(x):
    return x * jax.nn.sigmoid(x)


def _softplus(x):
    return jnp.maximum(x, 0.0) + jnp.log1p(jnp.exp(-jnp.abs(x)))


def _ln_in_kernel(x_ref, g_ref, b_ref, xf_ref, xb_ref):
    y = _ln(x_ref[...], g_ref[...], b_ref[...])
    xf_ref[...] = y
    xb_ref[...] = y.astype(BF16)


def _ln_in(x, g, b):
    t = x.shape[0]
    tm = _pick(t, (640, 512, 128))
    return pl.pallas_call(
        _ln_in_kernel,
        grid=(t // tm,),
        in_specs=[pl.BlockSpec((tm, D_MODEL), lambda i: (i, 0)),
                  pl.BlockSpec((1, D_MODEL), lambda i: (0, 0)),
                  pl.BlockSpec((1, D_MODEL), lambda i: (0, 0))],
        out_specs=[pl.BlockSpec((tm, D_MODEL), lambda i: (i, 0)),
                   pl.BlockSpec((tm, D_MODEL), lambda i: (i, 0))],
        out_shape=[jax.ShapeDtypeStruct((t, D_MODEL), F32), jax.ShapeDtypeStruct((t, D_MODEL), BF16)],
        compiler_params=_cp(("parallel",)),
        name="ln_in",
    )(x, g.reshape(1, -1), b.reshape(1, -1))


def _inproj_kernel(x_ref, w_ref, wdt_ref, h_ref, dt_ref):
    x = x_ref[...]
    h_ref[...] = jnp.dot(x, w_ref[...], preferred_element_type=F32)

    @pl.when(pl.program_id(1) == 0)
    def _():
        dt_ref[...] = jnp.dot(x, wdt_ref[...], preferred_element_type=F32)


def _inproj(xb, w_main, w_dt):
    t = xb.shape[0]
    tm = _pick(t, (1664, 1024, 128))
    tn = D_GROUP
    return pl.pallas_call(
        _inproj_kernel,
        grid=(t // tm, D_MAIN // tn),
        in_specs=[pl.BlockSpec((tm, D_MODEL), lambda i, j: (i, 0)),
                  pl.BlockSpec((D_MODEL, tn), lambda i, j: (0, j)),
                  pl.BlockSpec((D_MODEL, LANES), lambda i, j: (0, 0))],
        out_specs=[pl.BlockSpec((tm, tn), lambda i, j: (i, j)),
                   pl.BlockSpec((tm, LANES), lambda i, j: (i, 0))],
        out_shape=[jax.ShapeDtypeStruct((t, D_MAIN), F32), jax.ShapeDtypeStruct((t, LANES), F32)],
        compiler_params=_cp(("parallel", "arbitrary")),
        name="inproj",
    )(xb, w_main, w_dt)


def _rope_kernel(q_ref, k_ref, v_ref, c_ref, s1_ref, s2_ref, qb_ref, kf_ref, kb_ref, vb_ref):
    c = c_ref[...]
    s1 = s1_ref[...]
    s2 = s2_ref[...]

    def rot(x):
        return x * c + pltpu.roll(x, LANES - ROPE_DIM // 2, 1) * s1 + pltpu.roll(x, ROPE_DIM // 2, 1) * s2

    for h in range(DA_HEADS):
        sl = slice(h * LANES, (h + 1) * LANES)
        qr = rot(q_ref[:, sl])
        kr = rot(k_ref[:, sl])
        qb_ref[:, sl] = (qr * (DA_DIM ** -0.5)).astype(BF16)
        kf_ref[:, sl] = kr
        kb_ref[:, sl] = kr.astype(BF16)
    vb_ref[...] = v_ref[...].astype(BF16)


def _rope(h, cos_t, sin1_t, sin2_t):
    t = h.shape[0]
    tm = _pick(t, (640, 512, 128))
    hspec = lambda cb: pl.BlockSpec((tm, D_GROUP), lambda i, cb=cb: (i, cb))
    tspec = pl.BlockSpec((tm, LANES), lambda i: (i, 0))
    ospec = pl.BlockSpec((tm, D_GROUP), lambda i: (i, 0))
    return pl.pallas_call(
        _rope_kernel,
        grid=(t // tm,),
        in_specs=[hspec(CB_Q), hspec(CB_K), hspec(CB_V), tspec, tspec, tspec],
        out_specs=[ospec, ospec, ospec, ospec],
        out_shape=[jax.ShapeDtypeStruct((t, D_GROUP), BF16), jax.ShapeDtypeStruct((t, D_GROUP), F32),
                   jax.ShapeDtypeStruct((t, D_GROUP), BF16), jax.ShapeDtypeStruct((t, D_GROUP), BF16)],
        compiler_params=_cp(("parallel",)),
        name="rope",
    )(h, h, h, cos_t, sin1_t, sin2_t)


def _rope_tables(pos):
    half = ROPE_DIM // 2
    inv_freq = 1.0 / (ROPE_THETA ** (jnp.arange(half, dtype=F32) * 2.0 / ROPE_DIM))
    ang = pos.astype(F32)[:, None] * inv_freq
    cos, sin = jnp.cos(ang), jnp.sin(ang)
    n = pos.shape[0]
    one = jnp.ones((n, DA_DIM - ROPE_DIM), F32)
    zero = jnp.zeros((n, DA_DIM - ROPE_DIM), F32)
    z8 = jnp.zeros((n, half), F32)
    c = jnp.concatenate([cos, cos, one], 1)
    s1 = jnp.concatenate([-sin, z8, zero], 1)
    s2 = jnp.concatenate([z8, sin, zero], 1)
    return tuple(jnp.concatenate([a, a], 1) for a in (c, s1, s2))


def _ssd_kernel(z_ref, xs_ref, bc_ref, dt_ref, hist_ref, h0_ref, cw_ref, cb_ref, dtb_ref, a_ref, dsk_ref, ng_ref,
                y_ref, hist_o_ref, st_o_ref, xbuf, act, dts, ht, *, tl, q):
    hp = SSM_HEADS // SSM_GROUPS
    gw = hp * SSM_HEAD_DIM

    @pl.when(pl.program_id(1) == 0)
    def _():
        xbuf[0:SUBLANES, :] = hist_ref[0]
        ht[...] = h0_ref[0]

    xbuf[SUBLANES:SUBLANES + tl, 0:D_GROUP] = xs_ref[...]
    xbuf[SUBLANES:SUBLANES + tl, D_GROUP:2 * D_GROUP] = bc_ref[...]
    acc = cb_ref[...] + cw_ref[SSM_CONV - 1:SSM_CONV, :] * xbuf[SUBLANES:SUBLANES + tl, :]
    for k in range(SSM_CONV - 1):
        off = SUBLANES - (SSM_CONV - 1) + k
        acc = acc + cw_ref[k:k + 1, :] * xbuf[off:off + tl, :]
    act[...] = _silu(acc)
    tail = xbuf[tl:tl + SUBLANES, :]
    xbuf[0:SUBLANES, :] = tail
    hist_o_ref[0] = tail
    dts[...] = _softplus(dt_ref[...] + dtb_ref[...])

    ri = lax.broadcasted_iota(jnp.int32, (q, q), 0)
    ci = lax.broadcasted_iota(jnp.int32, (q, q), 1)
    causal = ri >= ci
    tril = causal.astype(F32)
    expand = (lax.broadcasted_iota(jnp.int32, (LANES, D_GROUP), 1) // SSM_HEAD_DIM
              == lax.broadcasted_iota(jnp.int32, (LANES, D_GROUP), 0)).astype(F32)
    eye8 = (lax.broadcasted_iota(jnp.int32, (SUBLANES, LANES), 0)
            == lax.broadcasted_iota(jnp.int32, (SUBLANES, LANES), 1)).astype(F32)
    lane_head = lax.broadcasted_iota(jnp.int32, (q, gw), 1) // SSM_HEAD_DIM

    def chunk(ci_, carry):
        r0 = pl.multiple_of(ci_ * q, q)
        dt_c = dts[pl.ds(r0, q), :]
        da = dt_c * a_ref[...]
        cs = jnp.dot(tril, da, precision=HI, preferred_element_type=F32)
        csx = jnp.dot(cs, expand, precision=HI, preferred_element_type=F32)
        dtx = jnp.dot(dt_c, expand, precision=HI, preferred_element_type=F32)
        cst = lax.dot_general(eye8, cs, NT, precision=HI, preferred_element_type=F32)
        dtt = lax.dot_general(eye8, dt_c, NT, precision=HI, preferred_element_type=F32)
        xs = act[pl.ds(r0, q), 0:D_GROUP]
        ecs = jnp.exp(csx)
        cs_last = csx[q - 1:q, :]
        xw = (xs * (jnp.exp(cs_last - csx) * dtx)).astype(BF16)
        xsb = xs.astype(BF16)
        dec = jnp.exp(cs_last)
        ys = []
        for g in range(SSM_GROUPS):
            gl = slice(g * gw, (g + 1) * gw)
            bg = act[pl.ds(r0, q), D_GROUP + g * SSM_STATE:D_GROUP + (g + 1) * SSM_STATE].astype(BF16)
            c0 = D_GROUP + SSM_GROUPS * SSM_STATE
            cg = act[pl.ds(r0, q), c0 + g * SSM_STATE:c0 + (g + 1) * SSM_STATE].astype(BF16)
            cb = lax.dot_general(cg, bg, NT, preferred_element_type=F32)
            hin = ht[:, gl]
            yg = jnp.dot(cg, hin.astype(BF16), preferred_element_type=F32) * ecs[:, gl]
            xg = xsb[:, gl]
            for hh in range(hp):
                h = g * hp + hh
                seg = cs[:, h:h + 1] - cst[h:h + 1, :]
                lm = jnp.exp(jnp.where(causal, seg, -jnp.inf))
                m = (cb * lm * dtt[h:h + 1, :]).astype(BF16)
                xm = jnp.where(lane_head == hh, xg, jnp.zeros_like(xg))
                yg = yg + jnp.dot(m, xm, preferred_element_type=F32)
            st = lax.dot_general(bg, xw[:, gl], TN, preferred_element_type=F32)
            ht[:, gl] = dec[:, gl] * hin + st
            ys.append(yg)
        y = jnp.concatenate(ys, axis=1) + dsk_ref[...] * xs
        y = y * _silu(z_ref[pl.ds(r0, q), :])
        outs = []
        for g in range(SSM_GROUPS):
            yg = y[:, g * gw:(g + 1) * gw]
            outs.append(yg * lax.rsqrt(jnp.mean(yg * yg, -1, keepdims=True) + EPS))
        y = jnp.concatenate(outs, axis=1) * ng_ref[...]
        y_ref[pl.ds(r0, q), :] = y.astype(y_ref.dtype)
        return carry

    lax.fori_loop(0, tl // q, chunk, 0)
    st_o_ref[0] = ht[...]


def _ssd(h, dt, hist, h0t, p, *, nb, tl, q, row0, out_rows, orow0=0):
    rb0 = row0 // tl
    ob0 = orow0 // tl
    nt = (p["len"]) // tl

    def hspec(cb):
        return pl.BlockSpec((tl, D_GROUP), lambda b, t, cb=cb: (rb0 + b * nt + t, cb))

    full = lambda shape: pl.BlockSpec(shape, lambda b, t: (0,) * len(shape))
    kern = functools.partial(_ssd_kernel, tl=tl, q=q)
    return pl.pallas_call(
        kern,
        grid=(nb, nt),
        in_specs=[hspec(CB_Z), hspec(CB_XS), hspec(CB_BC),
                  pl.BlockSpec((tl, LANES), lambda b, t: (rb0 + b * nt + t, 0)),
                  pl.BlockSpec((1, SUBLANES, SSM_CONV_CH), lambda b, t: (b, 0, 0)),
                  pl.BlockSpec((1, SSM_STATE, D_GROUP), lambda b, t: (b, 0, 0)),
                  full((SSM_CONV, SSM_CONV_CH)), full((1, SSM_CONV_CH)), full((1, LANES)), full((1, LANES)),
                  full((1, D_GROUP)), full((1, D_GROUP))],
        out_specs=[pl.BlockSpec((tl, D_GROUP), lambda b, t: (ob0 + b * nt + t, 0)),
                   pl.BlockSpec((1, SUBLANES, SSM_CONV_CH), lambda b, t: (b, 0, 0)),
                   pl.BlockSpec((1, SSM_STATE, D_GROUP), lambda b, t: (b, 0, 0))],
        out_shape=[jax.ShapeDtypeStruct((out_rows, D_GROUP), BF16),
                   jax.ShapeDtypeStruct((nb, SUBLANES, SSM_CONV_CH), F32),
                   jax.ShapeDtypeStruct((nb, SSM_STATE, D_GROUP), F32)],
        scratch_shapes=[pltpu.VMEM((tl + SUBLANES, SSM_CONV_CH), F32), pltpu.VMEM((tl, SSM_CONV_CH), F32),
                        pltpu.VMEM((tl, LANES), F32), pltpu.VMEM((SSM_STATE, D_GROUP), F32)],
        compiler_params=_cp(("parallel", "arbitrary")),
        name=p["name"],
    )(h, h, h, dt, hist, h0t, p["cw"], p["cb"], p["dtb"], p["a"], p["dsk"], p["ng"])


def _lane_fold(x, op):
    parts = [x[:, t * LANES:(t + 1) * LANES] for t in range(x.shape[1] // LANES)]
    while len(parts) > 1:
        parts = [op(parts[a], parts[a + 1]) for a in range(0, len(parts), 2)]
    return parts[0]


def _attn_kernel(it_ref, jt_ref, pt_ref, lam_ref, q_ref, k_ref, v_ref, g_ref, o_ref,
                 mx_ref, mrep_ref, ls_ref, acc_ref, *, tq, scale_out):
    step = pl.program_id(1)
    i = it_ref[step]
    j = jt_ref[step]
    sweep = pt_ref[step]
    nl = tq // LANES

    def scores(c, diagonal):
        sl = slice(c * DA_DIM, (c + 1) * DA_DIM)
        s = lax.dot_general(q_ref[:, sl], k_ref[:, sl], NT, preferred_element_type=F32)
        if diagonal:
            rowc = lax.broadcasted_iota(jnp.int32, (tq, tq), 0) // CHUNK
            colc = lax.broadcasted_iota(jnp.int32, (tq, tq), 1) // CHUNK
            s = jnp.where(colc <= rowc, s, -jnp.inf)
        return s

    def sweep0(diagonal):
        for c in range(2):
            mx_ref[c] = jnp.maximum(mx_ref[c], _lane_fold(scores(c, diagonal), jnp.maximum))
        if diagonal:
            for c in range(2):
                m = jnp.max(mx_ref[c], -1, keepdims=True)
                mrep_ref[c] = jnp.broadcast_to(m, (tq, LANES))
            ls_ref[...] = jnp.zeros(ls_ref.shape, F32)
            acc_ref[...] = jnp.zeros(acc_ref.shape, F32)

    def sweep1(diagonal):
        v = v_ref[...]
        for c in range(2):
            m = mrep_ref[c]
            p = jnp.exp(scores(c, diagonal) - jnp.concatenate([m] * nl, axis=1))
            ls_ref[c] = ls_ref[c] + _lane_fold(p, jnp.add)
            acc_ref[c] = acc_ref[c] + jnp.dot(p.astype(BF16), v, preferred_element_type=F32)
        if diagonal:
            l0 = jnp.sum(ls_ref[0], -1, keepdims=True)
            l1 = jnp.sum(ls_ref[1], -1, keepdims=True)
            o = acc_ref[0] / l0 - lam_ref[0] * (acc_ref[1] / l1)
            ms = jnp.mean(o * o, -1, keepdims=True)
            o_ref[...] = (o * lax.rsqrt(ms + EPS) * g_ref[...] * scale_out).astype(o_ref.dtype)

    @pl.when((sweep == 0) & (j == 0))
    def _():
        mx_ref[...] = jnp.full(mx_ref.shape, -jnp.inf, F32)

    for sw, fn in ((0, sweep0), (1, sweep1)):
        @pl.when((sweep == sw) & (j < i))
        def _(fn=fn):
            fn(False)

        @pl.when((sweep == sw) & (j == i))
        def _(fn=fn):
            fn(True)


def _attn_prompt(lam, qb, kb, vb, g, *, tp, out_rows, scale_out):
    tq = _pick(tp, (512, 256, 128))
    n = tp // tq
    steps = [(i, j, sw) for i in range(n) for sw in range(2) for j in range(i + 1)]
    it, jt, pt = (jnp.asarray([s[a] for s in steps], jnp.int32) for a in range(3))
    kern = functools.partial(_attn_kernel, tq=tq, scale_out=scale_out)
    grid_spec = pltpu.PrefetchScalarGridSpec(
        num_scalar_prefetch=3,
        grid=(DA_HEADS, len(steps)),
        in_specs=[pl.BlockSpec(memory_space=pltpu.SMEM),
                  pl.BlockSpec((tq, LANES), lambda h, s, it, jt, pt: (it[s], h)),
                  pl.BlockSpec((tq, LANES), lambda h, s, it, jt, pt: (jt[s], h)),
                  pl.BlockSpec((tq, LANES), lambda h, s, it, jt, pt: (jt[s] * pt[s], h)),
                  pl.BlockSpec((1, LANES), lambda h, s, it, jt, pt: (0, 0))],
        out_specs=pl.BlockSpec((tq, LANES), lambda h, s, it, jt, pt: (it[s], h)),
        scratch_shapes=[pltpu.VMEM((2, tq, LANES), F32), pltpu.VMEM((2, tq, LANES), F32),
                        pltpu.VMEM((2, tq, LANES), F32), pltpu.VMEM((2, tq, LANES), F32)],
    )
    return pl.pallas_call(
        kern,
        grid_spec=grid_spec,
        out_shape=jax.ShapeDtypeStruct((out_rows, D_GROUP), BF16),
        compiler_params=_cp(("parallel", "arbitrary")),
        name="attn_prompt",
    )(it, jt, pt, lam, qb, kb, vb, g)


def _attn_s_kernel(lam_ref, q_ref, kc_ref, vc_ref, kn_ref, vn_ref, g_ref, o_ref, *, scale_out):
    q = q_ref[...]
    kc = kc_ref[0].astype(BF16)
    vc = vc_ref[0].astype(BF16)
    kn = kn_ref[...]
    vn = vn_ref[...]
    outs = []
    for c in range(2):
        sl = slice(c * DA_DIM, (c + 1) * DA_DIM)
        sp = lax.dot_general(q[:, sl], kc[:, sl], NT, preferred_element_type=F32)
        sn = lax.dot_general(q[:, sl], kn[:, sl], NT, preferred_element_type=F32)
        m = jnp.maximum(jnp.max(sp, -1, keepdims=True), jnp.max(sn, -1, keepdims=True))
        pp = jnp.exp(sp - m)
        pn = jnp.exp(sn - m)
        l = jnp.sum(pp, -1, keepdims=True) + jnp.sum(pn, -1, keepdims=True)
        o = jnp.dot(pp.astype(BF16), vc, preferred_element_type=F32) + jnp.dot(pn.astype(BF16), vn,
                                                                               preferred_element_type=F32)
        outs.append(o / l)
    o = outs[0] - lam_ref[0] * outs[1]
    ms = jnp.mean(o * o, -1, keepdims=True)
    o_ref[...] = (o * lax.rsqrt(ms + EPS) * g_ref[...] * scale_out).astype(o_ref.dtype)


def _attn_sample(lam, qb, kb, vb, cache_k, cache_v, g, *, layer, nb, ls, row0, scale_out):
    past = cache_k.shape[1]
    rb0 = row0 // ls
    new = lambda: pl.BlockSpec((ls, LANES), lambda b, h: (rb0 + b, h))
    cache = lambda: pl.BlockSpec((1, past, LANES), lambda b, h: (layer * nb + b, 0, h))
    kern = functools.partial(_attn_s_kernel, scale_out=scale_out)
    return pl.pallas_call(
        kern,
        grid=(nb, DA_HEADS),
        in_specs=[pl.BlockSpec(memory_space=pltpu.SMEM),
                  new(), cache(), cache(), new(), new(), pl.BlockSpec((1, LANES), lambda b, h: (0, 0))],
        out_specs=pl.BlockSpec((ls, LANES), lambda b, h: (b, h)),
        out_shape=jax.ShapeDtypeStruct((nb * ls, D_GROUP), BF16),
        compiler_params=_cp(("parallel", "parallel")),
        name="attn_sample",
    )(lam, qb, cache_k, cache_v, kb, vb, g)


def _sgu_kernel(u_ref, v_ref, g_ref, b_ref, w_ref, bias_ref, y_ref, vo_ref, *, tl, q):
    v = _ln(v_ref[...], g_ref[...], b_ref[...])
    if vo_ref is not None:
        vo_ref[...] = v
    vb = v.astype(BF16)
    tri = lax.broadcasted_iota(jnp.int32, (q, q), 0) >= lax.broadcasted_iota(jnp.int32, (q, q), 1)
    for g in range(SG_GROUPS):
        w = jnp.where(tri, w_ref[g], 0.0).astype(BF16)
        sl = slice(g * LANES, (g + 1) * LANES)
        for c in range(tl // q):
            rs = slice(c * q, (c + 1) * q)
            s = jnp.dot(w, vb[rs, sl], preferred_element_type=F32) + bias_ref[:, sl]
            y_ref[rs, sl] = (u_ref[rs, sl] * s).astype(y_ref.dtype)


def _sgu(h, p, *, nrows, tl, q, row0, out_rows, want_v, orow0=0):
    rb0 = row0 // tl
    ob0 = orow0 // tl
    hspec = lambda cb: pl.BlockSpec((tl, D_GROUP), lambda i, cb=cb: (rb0 + i, cb))
    full = lambda shape: pl.BlockSpec(shape, lambda i: (0,) * len(shape))
    out_specs = [pl.BlockSpec((tl, D_GROUP), lambda i: (ob0 + i, 0))]
    out_shape = [jax.ShapeDtypeStruct((out_rows, D_GROUP), BF16)]
    if want_v:
        out_specs.append(pl.BlockSpec((tl, D_GROUP), lambda i: (i, 0)))
        out_shape.append(jax.ShapeDtypeStruct((nrows, D_GROUP), F32))
        kern = functools.partial(_sgu_kernel, tl=tl, q=q)
    else:
        kern = lambda u, v, g, b, w, bias, y: _sgu_kernel(u, v, g, b, w, bias, y, None, tl=tl, q=q)
    return pl.pallas_call(
        kern,
        grid=(nrows // tl,),
        in_specs=[hspec(CB_SU), hspec(CB_SV), full((1, D_GROUP)), full((1, D_GROUP)),
                  full((SG_GROUPS, q, q)), full((q, D_GROUP))],
        out_specs=out_specs,
        out_shape=out_shape,
        compiler_params=_cp(("parallel",)),
        name=p["name"],
    )(h, h, p["g"], p["b"], p["w"], p["bias"])


CV_HIST = 32
CV_RB = 64


def _cconv_kernel(a_ref, gt_ref, hist_ref, w_ref, b_ref, lg_ref, lb_ref, y_ref, tail_ref, buf, *, tl):
    @pl.when(pl.program_id(1) == 0)
    def _():
        buf[0:CV_HIST, :] = hist_ref[0]

    buf[CV_HIST:CV_HIST + tl, :] = a_ref[...] * jax.nn.sigmoid(gt_ref[...])
    rb = min(CV_RB, tl)
    pad = CV_HIST - (CV_WIDTH - 1)
    for r0 in range(0, tl, rb):
        acc = jnp.broadcast_to(b_ref[...], (rb, D_GROUP))
        for k in range(CV_WIDTH):
            acc = acc + w_ref[k:k + 1, :] * buf[r0 + k + pad:r0 + k + pad + rb, :]
        y_ref[r0:r0 + rb, :] = _silu(_ln(acc, lg_ref[...], lb_ref[...])).astype(y_ref.dtype)
    tail = buf[tl:tl + CV_HIST, :]
    buf[0:CV_HIST, :] = tail
    tail_ref[0] = tail


def _cconv(h, hist, p, *, nb, seqlen, tl, row0, out_rows, orow0=0):
    rb0 = row0 // tl
    ob0 = orow0 // tl
    nt = seqlen // tl
    hspec = lambda cb: pl.BlockSpec((tl, D_GROUP), lambda b, t, cb=cb: (rb0 + b * nt + t, cb))
    full = lambda shape: pl.BlockSpec(shape, lambda b, t: (0,) * len(shape))
    kern = functools.partial(_cconv_kernel, tl=tl)
    return pl.pallas_call(
        kern,
        grid=(nb, nt),
        in_specs=[hspec(CB_CA), hspec(CB_CG), pl.BlockSpec((1, CV_HIST, D_GROUP), lambda b, t: (b, 0, 0)),
                  full((CV_HIST, D_GROUP)), full((1, D_GROUP)), full((1, D_GROUP)), full((1, D_GROUP))],
        out_specs=[pl.BlockSpec((tl, D_GROUP), lambda b, t: (ob0 + b * nt + t, 0)),
                   pl.BlockSpec((1, CV_HIST, D_GROUP), lambda b, t: (b, 0, 0))],
        out_shape=[jax.ShapeDtypeStruct((out_rows, D_GROUP), BF16),
                   jax.ShapeDtypeStruct((nb, CV_HIST, D_GROUP), F32)],
        scratch_shapes=[pltpu.VMEM((tl + CV_HIST, D_GROUP), F32)],
        compiler_params=_cp(("parallel", "arbitrary")),
        name=p["name"],
    )(h, h, hist, p["w"], p["b"], p["lg"], p["lb"])


R_ROWS = 32


def _outproj_kernel(ya_ref, yb_ref, yc_ref, yd_ref, x_ref, w_ref, g_ref, b_ref, wr_ref, br_ref,
                    x1_ref, gcol_ref, eid_ref, cnt_ref, run_ref):
    @pl.when(pl.program_id(0) == 0)
    def _():
        run_ref[...] = jnp.zeros(run_ref.shape, F32)

    acc = jnp.dot(ya_ref[...], w_ref[0], preferred_element_type=F32)
    acc = acc + jnp.dot(yb_ref[...], w_ref[1], preferred_element_type=F32)
    acc = acc + jnp.dot(yc_ref[...], w_ref[2], preferred_element_type=F32)
    acc = acc + jnp.dot(yd_ref[...], w_ref[3], preferred_element_type=F32)
    x1 = _ln(ALPHA * x_ref[...] + acc, g_ref[...], b_ref[...])
    x1_ref[...] = x1
    tm = x1.shape[0]
    lt = lax.dot_general(wr_ref[...], x1, NT, precision=HI, preferred_element_type=F32) + br_ref[...]
    lg = lt[0:N_EGROUPS, :]
    ridx = lax.broadcasted_iota(jnp.int32, (N_EGROUPS, tm), 0).astype(F32)

    def softmax0(a):
        e = jnp.exp(a - jnp.max(a, 0, keepdims=True))
        return e / jnp.sum(e, 0, keepdims=True)

    def top1(pv):
        best = jnp.max(pv, 0, keepdims=True)
        return best, jnp.min(jnp.where(pv == best, ridx, float(N_EGROUPS)), 0, keepdims=True)

    p_g, g_idx = top1(softmax0(lg))
    le = jnp.zeros((EXP_PER_GROUP, tm), F32)
    for g in range(N_EGROUPS):
        r0 = SUBLANES + g * EXP_PER_GROUP
        le = le + jnp.where(g_idx == g, lt[r0:r0 + EXP_PER_GROUP, :], 0.0)
    pe = softmax0(le)
    p1, i1 = top1(pe)
    p2, i2 = top1(jnp.where(ridx == i1, -1.0, pe))
    den = p1 + p2
    e1 = g_idx * EXP_PER_GROUP + i1
    e2 = g_idx * EXP_PER_GROUP + i2
    srow = lax.broadcasted_iota(jnp.int32, (SUBLANES, tm), 0)
    gates8 = jnp.where(srow == 0, p_g * p1 / den, jnp.where(srow == 1, p_g * p2 / den, 0.0))
    eye8 = (lax.broadcasted_iota(jnp.int32, (SUBLANES, LANES), 0)
            == lax.broadcasted_iota(jnp.int32, (SUBLANES, LANES), 1)).astype(F32)
    gcol_ref[...] = lax.dot_general(gates8, eye8, TN, precision=HI, preferred_element_type=F32)
    eio = lax.broadcasted_iota(jnp.int32, (N_EXPERTS, tm), 0).astype(F32)
    sel0 = eio == e1
    sel1 = eio == e2
    before = (lax.broadcasted_iota(jnp.int32, (tm, tm), 0)
              < lax.broadcasted_iota(jnp.int32, (tm, tm), 1))
    before = jnp.where(before, 1.0, 0.0).astype(BF16)
    oh0 = jnp.where(sel0, 1.0, 0.0)
    oh1 = jnp.where(sel1, 1.0, 0.0)
    cum0 = jnp.dot(oh0.astype(BF16), before, preferred_element_type=F32)
    cum1 = jnp.dot(oh1.astype(BF16), before, preferred_element_type=F32)
    tot0 = jnp.sum(oh0, 1, keepdims=True)
    tot1 = jnp.sum(oh1, 1, keepdims=True)
    base = run_ref[...]
    r0 = jnp.sum(jnp.where(sel0, base + cum0, 0.0), 0, keepdims=True)
    r1 = jnp.sum(jnp.where(sel1, base + tot0 + cum1, 0.0), 0, keepdims=True)
    run_ref[...] = base + tot0 + tot1
    cnt_ref[...] = jnp.broadcast_to(base + tot0 + tot1, cnt_ref.shape)
    eid_ref[...] = jnp.zeros(eid_ref.shape, jnp.int32)
    eid_ref[0:1, :] = e1.astype(jnp.int32)
    eid_ref[1:2, :] = e2.astype(jnp.int32)
    eid_ref[2:3, :] = r0.astype(jnp.int32)
    eid_ref[3:4, :] = r1.astype(jnp.int32)


def _outproj(ys, x, w_out, g, b, wr, br):
    t = x.shape[0]
    tm = _pick(t, (640, 128))
    yspec = pl.BlockSpec((tm, D_GROUP), lambda i: (i, 0))
    full = lambda shape: pl.BlockSpec(shape, lambda i: (0,) * len(shape))
    return pl.pallas_call(
        _outproj_kernel,
        grid=(t // tm,),
        in_specs=[yspec, yspec, yspec, yspec, pl.BlockSpec((tm, D_MODEL), lambda i: (i, 0)),
                  full((4, D_GROUP, D_MODEL)), full((1, D_MODEL)), full((1, D_MODEL)),
                  full((R_ROWS, D_MODEL)), full((R_ROWS, 1))],
        out_specs=[pl.BlockSpec((tm, D_MODEL), lambda i: (i, 0)),
                   pl.BlockSpec((tm, LANES), lambda i: (i, 0)),
                   pl.BlockSpec((SUBLANES, tm), lambda i: (0, i)),
                   full((N_EXPERTS, LANES))],
        out_shape=[jax.ShapeDtypeStruct((t, D_MODEL), F32), jax.ShapeDtypeStruct((t, LANES), F32),
                   jax.ShapeDtypeStruct((SUBLANES, t), jnp.int32),
                   jax.ShapeDtypeStruct((N_EXPERTS, LANES), F32)],
        scratch_shapes=[pltpu.VMEM((N_EXPERTS, 1), F32)],
        compiler_params=_cp(("arbitrary",)),
        name="outproj_ln_router",
    )(*ys, x, w_out, g, b, wr, br)


def _moe_kernel(te_ref, nu_ref, x_ref, w1_ref, w3_ref, w2_ref, y_ref, w1b, w3b, w2b):
    i = pl.program_id(0)
    n_used = nu_ref[0]

    @pl.when(i < n_used)
    def _():
        @pl.when((i == 0) | (te_ref[i] != te_ref[jnp.maximum(i - 1, 0)]))
        def _():
            w1b[...] = w1_ref[0].astype(BF16)
            w3b[...] = w3_ref[0].astype(BF16)
            w2b[...] = w2_ref[0].astype(BF16)

        x = x_ref[...].astype(BF16)
        h1 = jnp.dot(x, w1b[...], preferred_element_type=F32)
        h3 = jnp.dot(x, w3b[...], preferred_element_type=F32)
        hh = (_silu(h1) * h3).astype(BF16)
        y_ref[...] = jnp.dot(hh, w2b[...], preferred_element_type=F32)

    @pl.when(i >= n_used)
    def _():
        y_ref[...] = jnp.zeros(y_ref.shape, F32)


def _moe(tile_expert, n_used, x_sorted, w1, w3, w2, *, n_tiles, layer):
    tm = MOE_TM
    xmap = lambda i, te, nu: (jnp.minimum(i, nu[0] - 1), 0)
    wmap = lambda i, te, nu: (layer * N_EXPERTS + te[i], 0, 0)
    grid_spec = pltpu.PrefetchScalarGridSpec(
        num_scalar_prefetch=2,
        grid=(n_tiles,),
        in_specs=[pl.BlockSpec((tm, D_MODEL), xmap),
                  pl.BlockSpec((1, D_MODEL, D_EXPERT), wmap),
                  pl.BlockSpec((1, D_MODEL, D_EXPERT), wmap),
                  pl.BlockSpec((1, D_EXPERT, D_MODEL), wmap)],
        out_specs=pl.BlockSpec((tm, D_MODEL), lambda i, te, nu: (i, 0)),
        scratch_shapes=[pltpu.VMEM((D_MODEL, D_EXPERT), BF16), pltpu.VMEM((D_MODEL, D_EXPERT), BF16),
                        pltpu.VMEM((D_EXPERT, D_MODEL), BF16)],
    )
    return pl.pallas_call(
        _moe_kernel,
        grid_spec=grid_spec,
        out_shape=jax.ShapeDtypeStruct((n_tiles * tm, D_MODEL), F32),
        compiler_params=_cp(("arbitrary",)),
        name="moe_grouped",
    )(tile_expert, n_used, x_sorted, w1, w3, w2)


def _disp_row_copy(x_ref, xs_hbm, sem, r, row):
    return pltpu.make_async_copy(x_ref.at[pl.ds(r, 1), :], xs_hbm.at[pl.ds(row, 1), :], sem)


def _disp_tile_copy(zbuf, xs_hbm, sem, tile, tm):
    return pltpu.make_async_copy(zbuf, xs_hbm.at[pl.ds(pl.multiple_of(tile * tm, tm), tm), :], sem)


def _dispatch_kernel(pos_ref, plo_ref, phi_ref, nu_ref, x_ref, xs_hbm, zbuf, sem, zsem, tsem, *, tm, t, mt, n_tiles):
    i = pl.program_id(0)

    def body(r, c):
        for k in range(2):
            _disp_row_copy(x_ref, xs_hbm, sem, r, pos_ref[k * t + i * tm + r]).start()
        return c
    lax.fori_loop(0, tm, body, 0)

    @pl.when(i == 0)
    def _():
        zbuf[...] = jnp.zeros(zbuf.shape, F32)
        for e in range(N_EXPERTS):
            def zbody(row, c):
                _disp_row_copy(zbuf, xs_hbm, zsem, 0, row).start()
                return c
            lax.fori_loop(plo_ref[e], phi_ref[e], zbody, 0)

        def tbody(tile, c):
            _disp_tile_copy(zbuf, xs_hbm, tsem, tile, mt).start()
            return c
        lax.fori_loop(nu_ref[0], n_tiles, tbody, 0)
        for e in range(N_EXPERTS):
            def zwait(row, c):
                _disp_row_copy(zbuf, xs_hbm, zsem, 0, row).wait()
                return c
            lax.fori_loop(plo_ref[e], phi_ref[e], zwait, 0)

        def twait(tile, c):
            _disp_tile_copy(zbuf, xs_hbm, tsem, tile, mt).wait()
            return c
        lax.fori_loop(nu_ref[0], n_tiles, twait, 0)

    def wbody(r, c):
        for k in range(2):
            _disp_row_copy(x_ref, xs_hbm, sem, r, 0).wait()
        return c
    lax.fori_loop(0, tm, wbody, 0)


def _dispatch(pos, pad_lo, pad_hi, n_used, x1, *, n_tiles):
    t = x1.shape[0]
    tm = _pick(t, (640, 128))
    mt = MOE_TM
    kern = functools.partial(_dispatch_kernel, tm=tm, t=t, mt=mt, n_tiles=n_tiles)
    grid_spec = pltpu.PrefetchScalarGridSpec(
        num_scalar_prefetch=4,
        grid=(t // tm,),
        in_specs=[pl.BlockSpec((tm, D_MODEL), lambda i, *_: (i, 0))],
        out_specs=pl.BlockSpec(memory_space=pl.ANY),
        scratch_shapes=[pltpu.VMEM((mt, D_MODEL), F32), pltpu.SemaphoreType.DMA(()),
                        pltpu.SemaphoreType.DMA(()), pltpu.SemaphoreType.DMA(())],
    )
    return pl.pallas_call(
        kern,
        grid_spec=grid_spec,
        out_shape=jax.ShapeDtypeStruct((n_tiles * mt, D_MODEL), F32),
        compiler_params=_cp(("arbitrary",)),
        name="moe_dispatch",
    )(pos, pad_lo, pad_hi, n_used, x1)


def _moe_plan(ids, cnt, n_tiles):
    tm = MOE_TM
    counts = cnt[:, 0].astype(jnp.int32)
    ntile_e = (counts + tm - 1) // tm
    tile_end = jnp.cumsum(ntile_e)
    row_start = (tile_end - ntile_e) * tm
    n_used = tile_end[-1]
    tiles = jnp.arange(n_tiles, dtype=jnp.int32)
    te = jnp.sum((tiles[:, None] >= tile_end[None, :]).astype(jnp.int32), 1)
    last_e = jnp.sum((n_used - 1 >= tile_end).astype(jnp.int32))
    te = jnp.where(tiles < n_used, te, last_e).astype(jnp.int32)
    eid, rank = ids[0:2], ids[2:4]
    sel = eid[:, :, None] == jnp.arange(N_EXPERTS, dtype=jnp.int32)
    pos = (jnp.sum(jnp.where(sel, row_start, 0), -1) + rank).reshape(-1).astype(jnp.int32)
    pad_lo = (row_start + counts).astype(jnp.int32)
    pad_hi = (row_start + ntile_e * tm).astype(jnp.int32)
    return te, n_used.reshape(1).astype(jnp.int32), pos, pad_lo, pad_hi


def _comb_row_copy(y_hbm, ybuf, sem, row, slot, k, r):
    return pltpu.make_async_copy(y_hbm.at[pl.ds(row, 1), :], ybuf.at[slot, k, pl.ds(r, 1), :], sem.at[slot])


def _combine_kernel(pos_ref, y_hbm, x1_ref, gc_ref, g_ref, b_ref, xf_ref, xb_ref, ybuf, sem, *, tm, t):
    i = pl.program_id(0)
    n = pl.num_programs(0)

    def start_tile(tile, slot):
        def body(r, c):
            for k in range(2):
                _comb_row_copy(y_hbm, ybuf, sem, pos_ref[k * t + tile * tm + r], slot, k, r).start()
            return c
        lax.fori_loop(0, tm, body, 0)

    @pl.when(i == 0)
    def _():
        start_tile(0, 0)

    @pl.when(i + 1 < n)
    def _():
        start_tile(i + 1, (i + 1) % 2)

    slot = i % 2

    def wbody(r, c):
        for k in range(2):
            _comb_row_copy(y_hbm, ybuf, sem, 0, slot, k, r).wait()
        return c
    lax.fori_loop(0, tm, wbody, 0)
    ffn = gc_ref[:, 0:1] * ybuf[slot, 0] + gc_ref[:, 1:2] * ybuf[slot, 1]
    x2 = _ln(ALPHA * x1_ref[...] + ffn, g_ref[...], b_ref[...])
    xf_ref[...] = x2
    xb_ref[...] = x2.astype(BF16)


def _combine(pos, y_sorted, x1, gcol, g, b):
    t = x1.shape[0]
    tm = _pick(t, (320, 128))
    kern = functools.partial(_combine_kernel, tm=tm, t=t)
    grid_spec = pltpu.PrefetchScalarGridSpec(
        num_scalar_prefetch=1,
        grid=(t // tm,),
        in_specs=[pl.BlockSpec(memory_space=pl.ANY),
                  pl.BlockSpec((tm, D_MODEL), lambda i, pos: (i, 0)),
                  pl.BlockSpec((tm, LANES), lambda i, pos: (i, 0)),
                  pl.BlockSpec((1, D_MODEL), lambda i, pos: (0, 0)),
                  pl.BlockSpec((1, D_MODEL), lambda i, pos: (0, 0))],
        out_specs=[pl.BlockSpec((tm, D_MODEL), lambda i, pos: (i, 0)),
                   pl.BlockSpec((tm, D_MODEL), lambda i, pos: (i, 0))],
        scratch_shapes=[pltpu.VMEM((2, 2, tm, D_MODEL), F32), pltpu.SemaphoreType.DMA((2,))],
    )
    return pl.pallas_call(
        kern,
        grid_spec=grid_spec,
        out_shape=[jax.ShapeDtypeStruct((t, D_MODEL), F32), jax.ShapeDtypeStruct((t, D_MODEL), BF16)],
        compiler_params=_cp(("arbitrary",)),
        name="moe_combine_ln",
    )(pos, y_sorted, x1, gcol, g, b)


def _row(v, n=None):
    v = v.reshape(1, -1).astype(F32)
    if n is not None and v.shape[1] < n:
        v = jnp.pad(v, ((0, 0), (0, n - v.shape[1])))
    return v


def kernel(x_prompt, x_sample, state_ssm_conv, state_ssm, cache_k, cache_v, state_conv, ln_in_g, ln_in_b, w_in, ssm_conv_w, ssm_conv_b, ssm_dt_bias, ssm_a_log, ssm_d, ssm_norm_g, da_lq1, da_lk1, da_lq2, da_lk2, da_norm_g, sg_ln_g, sg_ln_b, sg_w, sg_b, cv_w, cv_b, cv_ln_g, cv_ln_b, w_out, ln1_g, ln1_b, moe_wg_group, moe_bg_group, moe_wg_exp, moe_bg_exp, moe_w1, moe_w3, moe_w2, ln2_g, ln2_b):
    bp, tp, _ = x_prompt.shape
    nb, ls, _ = x_sample.shape
    depth = w_in.shape[0]
    past = cache_k.shape[2]
    assert bp == 1 and tp % 512 == 0 and ls % SUBLANES == 0 and ls <= CHUNK
    ts = nb * ls
    t = tp + ts
    n_tiles = (2 * t + N_EXPERTS * (MOE_TM - 1)) // MOE_TM + 1

    dt0 = D_GROUP + SSM_CONV_CH
    w1s = moe_w1.reshape(depth * N_EXPERTS, D_MODEL, D_EXPERT)
    w3s = moe_w3.reshape(depth * N_EXPERTS, D_MODEL, D_EXPERT)
    w2s = moe_w2.reshape(depth * N_EXPERTS, D_EXPERT, D_MODEL)
    wr = jnp.zeros((depth, R_ROWS, D_MODEL), F32)
    wr = wr.at[:, 0:N_EGROUPS].set(jnp.swapaxes(moe_wg_group, 1, 2))
    wr = wr.at[:, SUBLANES:SUBLANES + N_EXPERTS].set(jnp.swapaxes(moe_wg_exp, 1, 2))
    br = jnp.zeros((depth, R_ROWS, 1), F32)
    br = br.at[:, 0:N_EGROUPS, 0].set(moe_bg_group).at[:, SUBLANES:SUBLANES + N_EXPERTS, 0].set(moe_bg_exp)
    cache_k2 = cache_k.reshape(depth * nb, past, D_GROUP)
    cache_v2 = cache_v.reshape(depth * nb, past, D_GROUP)

    pos_all = jnp.concatenate([jnp.arange(tp), jnp.tile(past + jnp.arange(ls), nb)])
    cos_t, sin1_t, sin2_t = _rope_tables(pos_all)

    x_all = jnp.concatenate([x_prompt.reshape(tp, D_MODEL), x_sample.reshape(ts, D_MODEL)], 0)
    xf, xb = _ln_in(x_all, ln_in_g, ln_in_b)

    zeros_hist = jnp.zeros((1, SUBLANES, SSM_CONV_CH), F32)
    zeros_state = jnp.zeros((1, SSM_STATE, D_GROUP), F32)
    zeros_cv = jnp.zeros((1, CV_HIST, D_GROUP), F32)
    outs = {k: [] for k in ("p_sc", "p_ss", "p_k", "p_v", "p_cv", "s_sc", "s_ss", "s_k", "s_v", "s_cv", "s_sg")}
    tl_p = 512
    tl_c = 256

    for l in range(depth):
        lam_init = 0.8 - 0.6 * math.exp(-0.3 * l)
        w_in_l = w_in[l]
        w_main = jnp.concatenate([w_in_l[:, :dt0], w_in_l[:, dt0 + SSM_HEADS:]], -1).astype(BF16)
        w_dt = jnp.pad(w_in_l[:, dt0:dt0 + SSM_HEADS], ((0, 0), (0, LANES - SSM_HEADS))).astype(BF16)
        w_out_b = w_out[l].reshape(4, D_GROUP, D_MODEL).astype(BF16)
        h, dt = _inproj(xb, w_main, w_dt)

        ssd_p = dict(cw=ssm_conv_w[l], cb=_row(ssm_conv_b[l]), dtb=_row(ssm_dt_bias[l], LANES),
                     a=_row(-jnp.exp(ssm_a_log[l].astype(F32)), LANES),
                     dsk=_row(jnp.repeat(ssm_d[l], SSM_HEAD_DIM)), ng=_row(ssm_norm_g[l]))
        ya, hist_p, st_p = _ssd(h, dt, zeros_hist, zeros_state, dict(ssd_p, len=tp, name="ssd_prompt"),
                                nb=1, tl=tl_p, q=CHUNK, row0=0, out_rows=tp)
        hist_s_in = jnp.pad(state_ssm_conv[l], ((0, 0), (SUBLANES - (SSM_CONV - 1), 0), (0, 0)))
        h0t_s = jnp.swapaxes(state_ssm[l].reshape(nb, D_GROUP, SSM_STATE), 1, 2)
        ya_s, hist_s, st_s = _ssd(h, dt, hist_s_in, h0t_s, dict(ssd_p, len=ls, name="ssd_sample"),
                                  nb=nb, tl=ls, q=ls, row0=tp, out_rows=ts)
        ya = jnp.concatenate([ya, ya_s], 0)
        outs["p_sc"].append(hist_p[:, SUBLANES - (SSM_CONV - 1):])
        outs["s_sc"].append(hist_s[:, SUBLANES - (SSM_CONV - 1):])
        unt = lambda s: jnp.swapaxes(s, 1, 2).reshape(-1, SSM_HEADS, SSM_HEAD_DIM, SSM_STATE)
        outs["p_ss"].append(unt(st_p))
        outs["s_ss"].append(unt(st_s))

        qb, kf, kb, vb = _rope(h, cos_t, sin1_t, sin2_t)
        f32 = lambda a: a.astype(F32)
        lam = (jnp.exp(jnp.sum(f32(da_lq1[l]) * f32(da_lk1[l]))) - jnp.exp(jnp.sum(f32(da_lq2[l]) * f32(da_lk2[l])))
               + lam_init).reshape(1).astype(F32)
        gda = _row(da_norm_g[l])
        yb = _attn_prompt(lam, qb, kb, vb, gda, tp=tp, out_rows=tp, scale_out=1.0 - lam_init)
        yb_s = _attn_sample(lam, qb, kb, vb, cache_k2, cache_v2, gda, layer=l, nb=nb, ls=ls, row0=tp,
                            scale_out=1.0 - lam_init)
        yb = jnp.concatenate([yb, yb_s], 0)
        outs["p_k"].append(kf[:tp].reshape(1, tp, DA_HEADS, 2, DA_DIM))
        outs["s_k"].append(kf[tp:].reshape(nb, ls, DA_HEADS, 2, DA_DIM))
        vf = h[:, CB_V * D_GROUP:(CB_V + 1) * D_GROUP]
        outs["p_v"].append(vf[:tp].reshape(1, tp, DA_HEADS, DA_VDIM))
        outs["s_v"].append(vf[tp:].reshape(nb, ls, DA_HEADS, DA_VDIM))

        sg_common = dict(g=_row(sg_ln_g[l]), b=_row(sg_ln_b[l]))
        bias_full = lambda qq: jnp.repeat(sg_b[l][:, :qq].T, LANES, axis=1).astype(F32)
        (yc,) = _sgu(h, dict(sg_common, w=sg_w[l], bias=bias_full(SG_CHUNK), name="sgu_prompt"),
                     nrows=tp, tl=tl_p, q=SG_CHUNK, row0=0, out_rows=tp, want_v=False)
        yc_s, v_rows = _sgu(h, dict(sg_common, w=sg_w[l][:, :ls, :ls], bias=bias_full(ls), name="sgu_sample"),
                            nrows=ts, tl=ls, q=ls, row0=tp, out_rows=ts, want_v=True)
        yc = jnp.concatenate([yc, yc_s], 0)
        outs["s_sg"].append(v_rows.reshape(nb, ls, D_GROUP))

        cv_p = dict(w=jnp.pad(cv_w[l], ((0, CV_HIST - CV_WIDTH), (0, 0))), b=_row(cv_b[l]),
                    lg=_row(cv_ln_g[l]), lb=_row(cv_ln_b[l]))
        yd, tail_p = _cconv(h, zeros_cv, dict(cv_p, name="cconv_prompt"), nb=1, seqlen=tp, tl=tl_c, row0=0,
                            out_rows=tp)
        cv_hist_s = jnp.pad(state_conv[l], ((0, 0), (CV_HIST - (CV_WIDTH - 1), 0), (0, 0)))
        yd_s, tail_s = _cconv(h, cv_hist_s, dict(cv_p, name="cconv_sample"), nb=nb, seqlen=ls, tl=ls, row0=tp,
                              out_rows=ts)
        yd = jnp.concatenate([yd, yd_s], 0)
        outs["p_cv"].append(tail_p[:, CV_HIST - (CV_WIDTH - 1):])
        outs["s_cv"].append(tail_s[:, CV_HIST - (CV_WIDTH - 1):])

        x1, gcol, ids, cnt = _outproj((ya, yb, yc, yd), xf, w_out_b, _row(ln1_g[l]), _row(ln1_b[l]), wr[l], br[l])
        te, n_used, pos, pad_lo, pad_hi = _moe_plan(ids, cnt, n_tiles)
        x_sorted = _dispatch(pos, pad_lo, pad_hi, n_used, x1, n_tiles=n_tiles)
        y_sorted = _moe(te, n_used, x_sorted, w1s, w3s, w2s, n_tiles=n_tiles, layer=l)
        xf, xb = _combine(pos, y_sorted, x1, gcol, _row(ln2_g[l]), _row(ln2_b[l]))

    st = lambda k: jnp.stack(outs[k])
    return (xf[:tp].reshape(1, tp, D_MODEL), xf[tp:].reshape(nb, ls, D_MODEL),
            st("p_sc"), st("p_ss"), st("p_k"), st("p_v"), st("p_cv"),
            st("s_sc"), st("s_ss"), st("s_k"), st("s_v"), st("s_cv"), st("s_sg"))
```

```python
import functools
import math

import jax
import jax.numpy as jnp
from jax import lax
from jax.experimental import pallas as pl
from jax.experimental.pallas import tpu as pltpu

F32 = jnp.float32
BF16 = jnp.bfloat16
HI = lax.Precision.HIGHEST

D_MODEL = 2048
DEPTH = 4
CHUNK = 64
D_GROUP = 512
SSM_HEADS = 8
SSM_HEAD_DIM = 64
SSM_GROUPS = 2
SSM_STATE = 128
SSM_CONV = 4
SSM_CONV_CH = 1024
DA_HEADS = 4
DA_DIM = 64
DA_VDIM = 128
ROPE_DIM = 16
ROPE_THETA = 500000.0
SG_CHUNK = 128
SG_GROUPS = 4
CV_WIDTH = 31
N_EGROUPS = 4
EXP_PER_GROUP = 4
N_EXPERTS = 16
D_EXPERT = 512
ALPHA = (2 * DEPTH) ** 0.25
EPS = 1e-5

LANES = 128
SUBLANES = 8
VMEM_LIMIT = 56 * 1024 * 1024
D_MAIN = 10 * D_GROUP
MOE_TM = 256
DMA_UNROLL = 8
NT = (((1,), (1,)), ((), ()))
TN = (((0,), (0,)), ((), ()))

CB_Z, CB_XS, CB_BC, CB_Q, CB_K, CB_V, CB_SU, CB_SV, CB_CA, CB_CG = range(10)


def _cp(sem):
    return pltpu.CompilerParams(dimension_semantics=sem, vmem_limit_bytes=VMEM_LIMIT)


def _pick(n, cands):
    for c in cands:
        if n % c == 0:
            return c
    raise ValueError(f"no tile for {n}")


def _ln(x, g, b):
    mu = jnp.mean(x, -1, keepdims=True)
    xc = x - mu
    var = jnp.mean(xc * xc, -1, keepdims=True)
    return xc * lax.rsqrt(var + EPS) * g + b


def _silu(x):
    return x * jax.nn.sigmoid(x)


def _softplus(x):
    return jnp.maximum(x, 0.0) + jnp.log1p(jnp.exp(-jnp.abs(x)))


def _ln_in_kernel(x_ref, g_ref, b_ref, xf_ref, xb_ref):
    y = _ln(x_ref[...], g_ref[...], b_ref[...])
    xf_ref[...] = y
    xb_ref[...] = y.astype(BF16)


def _ln_in(x, g, b):
    t = x.shape[0]
    tm = _pick(t, (640, 512, 128))
    return pl.pallas_call(
        _ln_in_kernel,
        grid=(t // tm,),
        in_specs=[pl.BlockSpec((tm, D_MODEL), lambda i: (i, 0)),
                  pl.BlockSpec((1, D_MODEL), lambda i: (0, 0)),
                  pl.BlockSpec((1, D_MODEL), lambda i: (0, 0))],
        out_specs=[pl.BlockSpec((tm, D_MODEL), lambda i: (i, 0)),
                   pl.BlockSpec((tm, D_MODEL), lambda i: (i, 0))],
        out_shape=[jax.ShapeDtypeStruct((t, D_MODEL), F32), jax.ShapeDtypeStruct((t, D_MODEL), BF16)],
        compiler_params=_cp(("parallel",)),
        name="ln_in",
    )(x, g.reshape(1, -1), b.reshape(1, -1))


def _inproj_kernel(x_ref, w_ref, wdt_ref, h_ref, dt_ref):
    x = x_ref[...]
    h_ref[...] = jnp.dot(x, w_ref[...], preferred_element_type=F32)

    @pl.when(pl.program_id(1) == 0)
    def _():
        dt_ref[...] = jnp.dot(x, wdt_ref[...], preferred_element_type=F32)


def _inproj(xb, w_main, w_dt):
    t = xb.shape[0]
    tm = _pick(t, (1664, 1024, 128))
    tn = D_GROUP
    return pl.pallas_call(
        _inproj_kernel,
        grid=(t // tm, D_MAIN // tn),
        in_specs=[pl.BlockSpec((tm, D_MODEL), lambda i, j: (i, 0)),
                  pl.BlockSpec((D_MODEL, tn), lambda i, j: (0, j)),
                  pl.BlockSpec((D_MODEL, LANES), lambda i, j: (0, 0))],
        out_specs=[pl.BlockSpec((tm, tn), lambda i, j: (i, j)),
                   pl.BlockSpec((tm, LANES), lambda i, j: (i, 0))],
        out_shape=[jax.ShapeDtypeStruct((t, D_MAIN), F32), jax.ShapeDtypeStruct((t, LANES), F32)],
        compiler_params=_cp(("parallel", "arbitrary")),
        name="inproj",
    )(xb, w_main, w_dt)


def _rope_kernel(q_ref, k_ref, v_ref, c_ref, s1_ref, s2_ref, qb_ref, kf_ref, kb_ref, vb_ref):
    c = c_ref[...]
    s1 = s1_ref[...]
    s2 = s2_ref[...]

    def rot(x):
        return x * c + pltpu.roll(x, LANES - ROPE_DIM // 2, 1) * s1 + pltpu.roll(x, ROPE_DIM // 2, 1) * s2

    for h in range(DA_HEADS):
        sl = slice(h * LANES, (h + 1) * LANES)
        qr = rot(q_ref[:, sl])
        kr = rot(k_ref[:, sl])
        qb_ref[:, sl] = (qr * (DA_DIM ** -0.5)).astype(BF16)
        kf_ref[:, sl] = kr
        kb_ref[:, sl] = kr.astype(BF16)
    vb_ref[...] = v_ref[...].astype(BF16)


def _rope(h, cos_t, sin1_t, sin2_t):
    t = h.shape[0]
    tm = _pick(t, (640, 512, 128))
    hspec = lambda cb: pl.BlockSpec((tm, D_GROUP), lambda i, cb=cb: (i, cb))
    tspec = pl.BlockSpec((tm, LANES), lambda i: (i, 0))
    ospec = pl.BlockSpec((tm, D_GROUP), lambda i: (i, 0))
    return pl.pallas_call(
        _rope_kernel,
        grid=(t // tm,),
        in_specs=[hspec(CB_Q), hspec(CB_K), hspec(CB_V), tspec, tspec, tspec],
        out_specs=[ospec, ospec, ospec, ospec],
        out_shape=[jax.ShapeDtypeStruct((t, D_GROUP), BF16), jax.ShapeDtypeStruct((t, D_GROUP), F32),
                   jax.ShapeDtypeStruct((t, D_GROUP), BF16), jax.ShapeDtypeStruct((t, D_GROUP), BF16)],
        compiler_params=_cp(("parallel",)),
        name="rope",
    )(h, h, h, cos_t, sin1_t, sin2_t)


def _rope_tables(pos):
    half = ROPE_DIM // 2
    inv_freq = 1.0 / (ROPE_THETA ** (jnp.arange(half, dtype=F32) * 2.0 / ROPE_DIM))
    ang = pos.astype(F32)[:, None] * inv_freq
    cos, sin = jnp.cos(ang), jnp.sin(ang)
    n = pos.shape[0]
    one = jnp.ones((n, DA_DIM - ROPE_DIM), F32)
    zero = jnp.zeros((n, DA_DIM - ROPE_DIM), F32)
    z8 = jnp.zeros((n, half), F32)
    c = jnp.concatenate([cos, cos, one], 1)
    s1 = jnp.concatenate([-sin, z8, zero], 1)
    s2 = jnp.concatenate([z8, sin, zero], 1)
    return tuple(jnp.concatenate([a, a], 1) for a in (c, s1, s2))


def _ssd_kernel(z_ref, xs_ref, bc_ref, dt_ref, hist_ref, h0_ref, cw_ref, cb_ref, dtb_ref, a_ref, dsk_ref, ng_ref,
                y_ref, hist_o_ref, st_o_ref, xbuf, act, dts, ht, *, tl, q):
    hp = SSM_HEADS // SSM_GROUPS
    gw = hp * SSM_HEAD_DIM

    @pl.when(pl.program_id(1) == 0)
    def _():
        xbuf[0:SUBLANES, :] = hist_ref[0]
        ht[...] = h0_ref[0]

    xbuf[SUBLANES:SUBLANES + tl, 0:D_GROUP] = xs_ref[...]
    xbuf[SUBLANES:SUBLANES + tl, D_GROUP:2 * D_GROUP] = bc_ref[...]
    acc = cb_ref[...] + cw_ref[SSM_CONV - 1:SSM_CONV, :] * xbuf[SUBLANES:SUBLANES + tl, :]
    for k in range(SSM_CONV - 1):
        off = SUBLANES - (SSM_CONV - 1) + k
        acc = acc + cw_ref[k:k + 1, :] * xbuf[off:off + tl, :]
    act[...] = _silu(acc)
    tail = xbuf[tl:tl + SUBLANES, :]
    xbuf[0:SUBLANES, :] = tail
    hist_o_ref[0] = tail
    dts[...] = _softplus(dt_ref[...] + dtb_ref[...])

    ri = lax.broadcasted_iota(jnp.int32, (q, q), 0)
    ci = lax.broadcasted_iota(jnp.int32, (q, q), 1)
    causal = ri >= ci
    tril = causal.astype(F32)
    expand = (lax.broadcasted_iota(jnp.int32, (LANES, D_GROUP), 1) // SSM_HEAD_DIM
              == lax.broadcasted_iota(jnp.int32, (LANES, D_GROUP), 0)).astype(F32)
    eye8 = (lax.broadcasted_iota(jnp.int32, (SUBLANES, LANES), 0)
            == lax.broadcasted_iota(jnp.int32, (SUBLANES, LANES), 1)).astype(F32)
    lane_head = lax.broadcasted_iota(jnp.int32, (q, gw), 1) // SSM_HEAD_DIM

    def chunk(ci_, carry):
        r0 = pl.multiple_of(ci_ * q, q)
        dt_c = dts[pl.ds(r0, q), :]
        da = dt_c * a_ref[...]
        cs = jnp.dot(tril, da, precision=HI, preferred_element_type=F32)
        csx = jnp.dot(cs, expand, precision=HI, preferred_element_type=F32)
        dtx = jnp.dot(dt_c, expand, precision=HI, preferred_element_type=F32)
        cst = lax.dot_general(eye8, cs, NT, precision=HI, preferred_element_type=F32)
        dtt = lax.dot_general(eye8, dt_c, NT, precision=HI, preferred_element_type=F32)
        xs = act[pl.ds(r0, q), 0:D_GROUP]
        ecs = jnp.exp(csx)
        cs_last = csx[q - 1:q, :]
        xw = (xs * (jnp.exp(cs_last - csx) * dtx)).astype(BF16)
        xsb = xs.astype(BF16)
        dec = jnp.exp(cs_last)
        ys = []
        for g in range(SSM_GROUPS):
            gl = slice(g * gw, (g + 1) * gw)
            bg = act[pl.ds(r0, q), D_GROUP + g * SSM_STATE:D_GROUP + (g + 1) * SSM_STATE].astype(BF16)
            c0 = D_GROUP + SSM_GROUPS * SSM_STATE
            cg = act[pl.ds(r0, q), c0 + g * SSM_STATE:c0 + (g + 1) * SSM_STATE].astype(BF16)
            cb = lax.dot_general(cg, bg, NT, preferred_element_type=F32)
            hin = ht[:, gl]
            yg = jnp.dot(cg, hin.astype(BF16), preferred_element_type=F32) * ecs[:, gl]
            xg = xsb[:, gl]
            for hh in range(hp):
                h = g * hp + hh
                seg = cs[:, h:h + 1] - cst[h:h + 1, :]
                lm = jnp.exp(jnp.where(causal, seg, -jnp.inf))
                m = (cb * lm * dtt[h:h + 1, :]).astype(BF16)
                xm = jnp.where(lane_head == hh, xg, jnp.zeros_like(xg))
                yg = yg + jnp.dot(m, xm, preferred_element_type=F32)
            st = lax.dot_general(bg, xw[:, gl], TN, preferred_element_type=F32)
            ht[:, gl] = dec[:, gl] * hin + st
            ys.append(yg)
        y = jnp.concatenate(ys, axis=1) + dsk_ref[...] * xs
        y = y * _silu(z_ref[pl.ds(r0, q), :])
        outs = []
        for g in range(SSM_GROUPS):
            yg = y[:, g * gw:(g + 1) * gw]
            outs.append(yg * lax.rsqrt(jnp.mean(yg * yg, -1, keepdims=True) + EPS))
        y = jnp.concatenate(outs, axis=1) * ng_ref[...]
        y_ref[pl.ds(r0, q), :] = y.astype(y_ref.dtype)
        return carry

    lax.fori_loop(0, tl // q, chunk, 0)
    st_o_ref[0] = ht[...]


def _ssd(h, dt, hist, h0t, p, *, nb, tl, q, row0, out_rows, orow0=0):
    rb0 = row0 // tl
    ob0 = orow0 // tl
    nt = (p["len"]) // tl

    def hspec(cb):
        return pl.BlockSpec((tl, D_GROUP), lambda b, t, cb=cb: (rb0 + b * nt + t, cb))

    full = lambda shape: pl.BlockSpec(shape, lambda b, t: (0,) * len(shape))
    kern = functools.partial(_ssd_kernel, tl=tl, q=q)
    return pl.pallas_call(
        kern,
        grid=(nb, nt),
        in_specs=[hspec(CB_Z), hspec(CB_XS), hspec(CB_BC),
                  pl.BlockSpec((tl, LANES), lambda b, t: (rb0 + b * nt + t, 0)),
                  pl.BlockSpec((1, SUBLANES, SSM_CONV_CH), lambda b, t: (b, 0, 0)),
                  pl.BlockSpec((1, SSM_STATE, D_GROUP), lambda b, t: (b, 0, 0)),
                  full((SSM_CONV, SSM_CONV_CH)), full((1, SSM_CONV_CH)), full((1, LANES)), full((1, LANES)),
                  full((1, D_GROUP)), full((1, D_GROUP))],
        out_specs=[pl.BlockSpec((tl, D_GROUP), lambda b, t: (ob0 + b * nt + t, 0)),
                   pl.BlockSpec((1, SUBLANES, SSM_CONV_CH), lambda b, t: (b, 0, 0)),
                   pl.BlockSpec((1, SSM_STATE, D_GROUP), lambda b, t: (b, 0, 0))],
        out_shape=[jax.ShapeDtypeStruct((out_rows, D_GROUP), BF16),
                   jax.ShapeDtypeStruct((nb, SUBLANES, SSM_CONV_CH), F32),
                   jax.ShapeDtypeStruct((nb, SSM_STATE, D_GROUP), F32)],
        scratch_shapes=[pltpu.VMEM((tl + SUBLANES, SSM_CONV_CH), F32), pltpu.VMEM((tl, SSM_CONV_CH), F32),
                        pltpu.VMEM((tl, LANES), F32), pltpu.VMEM((SSM_STATE, D_GROUP), F32)],
        compiler_params=_cp(("parallel", "arbitrary")),
        name=p["name"],
    )(h, h, h, dt, hist, h0t, p["cw"], p["cb"], p["dtb"], p["a"], p["dsk"], p["ng"])


def _lane_fold(x, op):
    parts = [x[:, t * LANES:(t + 1) * LANES] for t in range(x.shape[1] // LANES)]
    while len(parts) > 1:
        parts = [op(parts[a], parts[a + 1]) for a in range(0, len(parts), 2)]
    return parts[0]


def _attn_kernel(it_ref, jt_ref, pt_ref, lam_ref, q_ref, k_ref, v_ref, g_ref, o_ref,
                 mx_ref, mrep_ref, ls_ref, acc_ref, *, tq, scale_out):
    step = pl.program_id(1)
    i = it_ref[step]
    j = jt_ref[step]
    sweep = pt_ref[step]
    nl = tq // LANES

    def scores(c, diagonal):
        sl = slice(c * DA_DIM, (c + 1) * DA_DIM)
        s = lax.dot_general(q_ref[:, sl], k_ref[:, sl], NT, preferred_element_type=F32)
        if diagonal:
            rowc = lax.broadcasted_iota(jnp.int32, (tq, tq), 0) // CHUNK
            colc = lax.broadcasted_iota(jnp.int32, (tq, tq), 1) // CHUNK
            s = jnp.where(colc <= rowc, s, -jnp.inf)
        return s

    def sweep0(diagonal):
        for c in range(2):
            mx_ref[c] = jnp.maximum(mx_ref[c], _lane_fold(scores(c, diagonal), jnp.maximum))
        if diagonal:
            for c in range(2):
                m = jnp.max(mx_ref[c], -1, keepdims=True)
                mrep_ref[c] = jnp.broadcast_to(m, (tq, LANES))
            ls_ref[...] = jnp.zeros(ls_ref.shape, F32)
            acc_ref[...] = jnp.zeros(acc_ref.shape, F32)

    def sweep1(diagonal):
        v = v_ref[...]
        for c in range(2):
            m = mrep_ref[c]
            p = jnp.exp(scores(c, diagonal) - jnp.concatenate([m] * nl, axis=1))
            ls_ref[c] = ls_ref[c] + _lane_fold(p, jnp.add)
            acc_ref[c] = acc_ref[c] + jnp.dot(p.astype(BF16), v, preferred_element_type=F32)
        if diagonal:
            l0 = jnp.sum(ls_ref[0], -1, keepdims=True)
            l1 = jnp.sum(ls_ref[1], -1, keepdims=True)
            o = acc_ref[0] / l0 - lam_ref[0] * (acc_ref[1] / l1)
            ms = jnp.mean(o * o, -1, keepdims=True)
            o_ref[...] = (o * lax.rsqrt(ms + EPS) * g_ref[...] * scale_out).astype(o_ref.dtype)

    @pl.when((sweep == 0) & (j == 0))
    def _():
        mx_ref[...] = jnp.full(mx_ref.shape, -jnp.inf, F32)

    for sw, fn in ((0, sweep0), (1, sweep1)):
        @pl.when((sweep == sw) & (j < i))
        def _(fn=fn):
            fn(False)

        @pl.when((sweep == sw) & (j == i))
        def _(fn=fn):
            fn(True)


def _attn_prompt(lam, qb, kb, vb, g, *, tp, out_rows, scale_out):
    tq = _pick(tp, (1024, 512, 256, 128))
    n = tp // tq
    steps = [(i, j, sw) for i in range(n) for sw in range(2) for j in range(i + 1)]
    it, jt, pt = (jnp.asarray([s[a] for s in steps], jnp.int32) for a in range(3))
    kern = functools.partial(_attn_kernel, tq=tq, scale_out=scale_out)
    grid_spec = pltpu.PrefetchScalarGridSpec(
        num_scalar_prefetch=3,
        grid=(DA_HEADS, len(steps)),
        in_specs=[pl.BlockSpec(memory_space=pltpu.SMEM),
                  pl.BlockSpec((tq, LANES), lambda h, s, it, jt, pt: (it[s], h)),
                  pl.BlockSpec((tq, LANES), lambda h, s, it, jt, pt: (jt[s], h)),
                  pl.BlockSpec((tq, LANES), lambda h, s, it, jt, pt: (jt[s] * pt[s], h)),
                  pl.BlockSpec((1, LANES), lambda h, s, it, jt, pt: (0, 0))],
        out_specs=pl.BlockSpec((tq, LANES), lambda h, s, it, jt, pt: (it[s], h)),
        scratch_shapes=[pltpu.VMEM((2, tq, LANES), F32), pltpu.VMEM((2, tq, LANES), F32),
                        pltpu.VMEM((2, tq, LANES), F32), pltpu.VMEM((2, tq, LANES), F32)],
    )
    return pl.pallas_call(
        kern,
        grid_spec=grid_spec,
        out_shape=jax.ShapeDtypeStruct((out_rows, D_GROUP), BF16),
        compiler_params=_cp(("parallel", "arbitrary")),
        name="attn_prompt",
    )(it, jt, pt, lam, qb, kb, vb, g)


def _attn_s_kernel(lam_ref, q_ref, kc_ref, vc_ref, kn_ref, vn_ref, g_ref, o_ref, *, scale_out):
    q = q_ref[...]
    kc = kc_ref[0].astype(BF16)
    vc = vc_ref[0].astype(BF16)
    kn = kn_ref[...]
    vn = vn_ref[...]
    outs = []
    for c in range(2):
        sl = slice(c * DA_DIM, (c + 1) * DA_DIM)
        sp = lax.dot_general(q[:, sl], kc[:, sl], NT, preferred_element_type=F32)
        sn = lax.dot_general(q[:, sl], kn[:, sl], NT, preferred_element_type=F32)
        m = jnp.maximum(jnp.max(sp, -1, keepdims=True), jnp.max(sn, -1, keepdims=True))
        pp = jnp.exp(sp - m)
        pn = jnp.exp(sn - m)
        l = jnp.sum(pp, -1, keepdims=True) + jnp.sum(pn, -1, keepdims=True)
        o = jnp.dot(pp.astype(BF16), vc, preferred_element_type=F32) + jnp.dot(pn.astype(BF16), vn,
                                                                               preferred_element_type=F32)
        outs.append(o / l)
    o = outs[0] - lam_ref[0] * outs[1]
    ms = jnp.mean(o * o, -1, keepdims=True)
    o_ref[...] = (o * lax.rsqrt(ms + EPS) * g_ref[...] * scale_out).astype(o_ref.dtype)


def _attn_sample(lam, qb, kb, vb, cache_k, cache_v, g, *, layer, nb, ls, row0, scale_out):
    past = cache_k.shape[1]
    rb0 = row0 // ls
    new = lambda: pl.BlockSpec((ls, LANES), lambda b, h: (rb0 + b, h))
    cache = lambda: pl.BlockSpec((1, past, LANES), lambda b, h: (layer * nb + b, 0, h))
    kern = functools.partial(_attn_s_kernel, scale_out=scale_out)
    return pl.pallas_call(
        kern,
        grid=(nb, DA_HEADS),
        in_specs=[pl.BlockSpec(memory_space=pltpu.SMEM),
                  new(), cache(), cache(), new(), new(), pl.BlockSpec((1, LANES), lambda b, h: (0, 0))],
        out_specs=pl.BlockSpec((ls, LANES), lambda b, h: (b, h)),
        out_shape=jax.ShapeDtypeStruct((nb * ls, D_GROUP), BF16),
        compiler_params=_cp(("parallel", "parallel")),
        name="attn_sample",
    )(lam, qb, cache_k, cache_v, kb, vb, g)


def _sgu_kernel(u_ref, v_ref, g_ref, b_ref, w_ref, bias_ref, y_ref, vo_ref, *, tl, q):
    v = _ln(v_ref[...], g_ref[...], b_ref[...])
    if vo_ref is not None:
        vo_ref[...] = v
    vb = v.astype(BF16)
    tri = lax.broadcasted_iota(jnp.int32, (q, q), 0) >= lax.broadcasted_iota(jnp.int32, (q, q), 1)
    for g in range(SG_GROUPS):
        w = jnp.where(tri, w_ref[g], 0.0).astype(BF16)
        sl = slice(g * LANES, (g + 1) * LANES)
        for c in range(tl // q):
            rs = slice(c * q, (c + 1) * q)
            s = jnp.dot(w, vb[rs, sl], preferred_element_type=F32) + bias_ref[:, sl]
            y_ref[rs, sl] = (u_ref[rs, sl] * s).astype(y_ref.dtype)


def _sgu(h, p, *, nrows, tl, q, row0, out_rows, want_v, orow0=0):
    rb0 = row0 // tl
    ob0 = orow0 // tl
    hspec = lambda cb: pl.BlockSpec((tl, D_GROUP), lambda i, cb=cb: (rb0 + i, cb))
    full = lambda shape: pl.BlockSpec(shape, lambda i: (0,) * len(shape))
    out_specs = [pl.BlockSpec((tl, D_GROUP), lambda i: (ob0 + i, 0))]
    out_shape = [jax.ShapeDtypeStruct((out_rows, D_GROUP), BF16)]
    if want_v:
        out_specs.append(pl.BlockSpec((tl, D_GROUP), lambda i: (i, 0)))
        out_shape.append(jax.ShapeDtypeStruct((nrows, D_GROUP), F32))
        kern = functools.partial(_sgu_kernel, tl=tl, q=q)
    else:
        kern = lambda u, v, g, b, w, bias, y: _sgu_kernel(u, v, g, b, w, bias, y, None, tl=tl, q=q)
    return pl.pallas_call(
        kern,
        grid=(nrows // tl,),
        in_specs=[hspec(CB_SU), hspec(CB_SV), full((1, D_GROUP)), full((1, D_GROUP)),
                  full((SG_GROUPS, q, q)), full((q, D_GROUP))],
        out_specs=out_specs,
        out_shape=out_shape,
        compiler_params=_cp(("parallel",)),
        name=p["name"],
    )(h, h, p["g"], p["b"], p["w"], p["bias"])


CV_HIST = 32
CV_RB = 64


def _cconv_kernel(a_ref, gt_ref, hist_ref, w_ref, b_ref, lg_ref, lb_ref, y_ref, tail_ref, buf, sh, *, tl):
    @pl.when(pl.program_id(1) == 0)
    def _():
        buf[0:CV_HIST, :] = hist_ref[0]

    buf[CV_HIST:CV_HIST + tl, :] = a_ref[...] * jax.nn.sigmoid(gt_ref[...])
    n_sh = tl + CV_HIST - SUBLANES
    for r in range(1, SUBLANES):
        sh[r - 1, 0:n_sh, :] = buf[r:r + n_sh, :]
    rb = min(CV_RB, tl)
    pad = CV_HIST - (CV_WIDTH - 1)
    for r0 in range(0, tl, rb):
        acc = jnp.broadcast_to(b_ref[...], (rb, D_GROUP))
        for k in range(CV_WIDTH):
            m, r = divmod(k + pad, SUBLANES)
            lo = r0 + m * SUBLANES
            tap = buf[lo:lo + rb, :] if r == 0 else sh[r - 1, lo:lo + rb, :]
            acc = acc + w_ref[k:k + 1, :] * tap
        y_ref[r0:r0 + rb, :] = _silu(_ln(acc, lg_ref[...], lb_ref[...])).astype(y_ref.dtype)
    tail = buf[tl:tl + CV_HIST, :]
    buf[0:CV_HIST, :] = tail
    tail_ref[0] = tail


def _cconv(h, hist, p, *, nb, seqlen, tl, row0, out_rows, orow0=0):
    rb0 = row0 // tl
    ob0 = orow0 // tl
    nt = seqlen // tl
    hspec = lambda cb: pl.BlockSpec((tl, D_GROUP), lambda b, t, cb=cb: (rb0 + b * nt + t, cb))
    full = lambda shape: pl.BlockSpec(shape, lambda b, t: (0,) * len(shape))
    kern = functools.partial(_cconv_kernel, tl=tl)
    return pl.pallas_call(
        kern,
        grid=(nb, nt),
        in_specs=[hspec(CB_CA), hspec(CB_CG), pl.BlockSpec((1, CV_HIST, D_GROUP), lambda b, t: (b, 0, 0)),
                  full((CV_HIST, D_GROUP)), full((1, D_GROUP)), full((1, D_GROUP)), full((1, D_GROUP))],
        out_specs=[pl.BlockSpec((tl, D_GROUP), lambda b, t: (ob0 + b * nt + t, 0)),
                   pl.BlockSpec((1, CV_HIST, D_GROUP), lambda b, t: (b, 0, 0))],
        out_shape=[jax.ShapeDtypeStruct((out_rows, D_GROUP), BF16),
                   jax.ShapeDtypeStruct((nb, CV_HIST, D_GROUP), F32)],
        scratch_shapes=[pltpu.VMEM((tl + CV_HIST, D_GROUP), F32),
                        pltpu.VMEM((SUBLANES - 1, tl + CV_HIST - SUBLANES, D_GROUP), F32)],
        compiler_params=_cp(("parallel", "arbitrary")),
        name=p["name"],
    )(h, h, hist, p["w"], p["b"], p["lg"], p["lb"])


R_ROWS = 32


def _outproj_kernel(*refs, n_full, rem):
    yp_refs, ys_refs = refs[0:4], refs[4:8]
    (x_ref, w_ref, g_ref, b_ref, wr_ref, br_ref, x1_ref, gcol_ref, eid_ref, cnt_ref, run_ref, ym_ref) = refs[8:]
    i = pl.program_id(0)
    tm = x_ref.shape[0]

    @pl.when(i == 0)
    def _():
        run_ref[...] = jnp.zeros(run_ref.shape, F32)

    @pl.when(i < n_full)
    def _():
        for m in range(4):
            ym_ref[m] = yp_refs[m][...]

    @pl.when(i >= n_full)
    def _():
        for m in range(4):
            if rem:
                ym_ref[m, 0:rem, :] = yp_refs[m][0:rem, :]
            ym_ref[m, rem:tm, :] = ys_refs[m][...]

    acc = jnp.dot(ym_ref[0], w_ref[0], preferred_element_type=F32)
    for m in range(1, 4):
        acc = acc + jnp.dot(ym_ref[m], w_ref[m], preferred_element_type=F32)
    x1 = _ln(ALPHA * x_ref[...] + acc, g_ref[...], b_ref[...])
    x1_ref[...] = x1
    lt = lax.dot_general(wr_ref[...], x1, NT, precision=HI, preferred_element_type=F32) + br_ref[...]
    lg = lt[0:N_EGROUPS, :]
    ridx = lax.broadcasted_iota(jnp.int32, (N_EGROUPS, tm), 0).astype(F32)

    def softmax0(a):
        e = jnp.exp(a - jnp.max(a, 0, keepdims=True))
        return e / jnp.sum(e, 0, keepdims=True)

    def top1(pv):
        best = jnp.max(pv, 0, keepdims=True)
        return best, jnp.min(jnp.where(pv == best, ridx, float(N_EGROUPS)), 0, keepdims=True)

    p_g, g_idx = top1(softmax0(lg))
    le = jnp.zeros((EXP_PER_GROUP, tm), F32)
    for g in range(N_EGROUPS):
        r0 = SUBLANES + g * EXP_PER_GROUP
        le = le + jnp.where(g_idx == g, lt[r0:r0 + EXP_PER_GROUP, :], 0.0)
    pe = softmax0(le)
    p1, i1 = top1(pe)
    p2, i2 = top1(jnp.where(ridx == i1, -1.0, pe))
    den = p1 + p2
    e1 = g_idx * EXP_PER_GROUP + i1
    e2 = g_idx * EXP_PER_GROUP + i2
    srow = lax.broadcasted_iota(jnp.int32, (SUBLANES, tm), 0)
    gates8 = jnp.where(srow == 0, p_g * p1 / den, jnp.where(srow == 1, p_g * p2 / den, 0.0))
    eye8 = (lax.broadcasted_iota(jnp.int32, (SUBLANES, LANES), 0)
            == lax.broadcasted_iota(jnp.int32, (SUBLANES, LANES), 1)).astype(F32)
    gcol_ref[...] = lax.dot_general(gates8, eye8, TN, precision=HI, preferred_element_type=F32)
    eio = lax.broadcasted_iota(jnp.int32, (N_EXPERTS, tm), 0).astype(F32)
    sel0 = eio == e1
    sel1 = eio == e2
    before = (lax.broadcasted_iota(jnp.int32, (tm, tm), 0)
              < lax.broadcasted_iota(jnp.int32, (tm, tm), 1))
    before = jnp.where(before, 1.0, 0.0).astype(BF16)
    oh0 = jnp.where(sel0, 1.0, 0.0)
    oh1 = jnp.where(sel1, 1.0, 0.0)
    cum0 = jnp.dot(oh0.astype(BF16), before, preferred_element_type=F32)
    cum1 = jnp.dot(oh1.astype(BF16), before, preferred_element_type=F32)
    tot0 = jnp.sum(oh0, 1, keepdims=True)
    tot1 = jnp.sum(oh1, 1, keepdims=True)
    base = run_ref[...]
    r0 = jnp.sum(jnp.where(sel0, base + cum0, 0.0), 0, keepdims=True)
    r1 = jnp.sum(jnp.where(sel1, base + tot0 + cum1, 0.0), 0, keepdims=True)
    run_ref[...] = base + tot0 + tot1
    cnt_ref[...] = jnp.broadcast_to(base + tot0 + tot1, cnt_ref.shape)
    eid_ref[...] = jnp.zeros(eid_ref.shape, jnp.int32)
    eid_ref[0:1, :] = e1.astype(jnp.int32)
    eid_ref[1:2, :] = e2.astype(jnp.int32)
    eid_ref[2:3, :] = r0.astype(jnp.int32)
    eid_ref[3:4, :] = r1.astype(jnp.int32)


def _outproj(ys_prompt, ys_sample, x, w_out, g, b, wr, br):
    t = x.shape[0]
    tp, ts = ys_prompt[0].shape[0], ys_sample[0].shape[0]
    tm = _pick(t, (640, 128))
    n_full, rem = divmod(tp, tm)
    assert rem + ts == tm and t == tp + ts, "sample rows must complete the last token tile"
    last_pb = pl.cdiv(tp, tm) - 1
    yspec = pl.BlockSpec((tm, D_GROUP), lambda i: (jnp.minimum(i, last_pb), 0))
    sspec = pl.BlockSpec((ts, D_GROUP), lambda i: (0, 0))
    full = lambda shape: pl.BlockSpec(shape, lambda i: (0,) * len(shape))
    kern = functools.partial(_outproj_kernel, n_full=n_full, rem=rem)
    return pl.pallas_call(
        kern,
        grid=(t // tm,),
        in_specs=[yspec, yspec, yspec, yspec, sspec, sspec, sspec, sspec,
                  pl.BlockSpec((tm, D_MODEL), lambda i: (i, 0)),
                  full((4, D_GROUP, D_MODEL)), full((1, D_MODEL)), full((1, D_MODEL)),
                  full((R_ROWS, D_MODEL)), full((R_ROWS, 1))],
        out_specs=[pl.BlockSpec((tm, D_MODEL), lambda i: (i, 0)),
                   pl.BlockSpec((tm, LANES), lambda i: (i, 0)),
                   pl.BlockSpec((SUBLANES, tm), lambda i: (0, i)),
                   full((N_EXPERTS, LANES))],
        out_shape=[jax.ShapeDtypeStruct((t, D_MODEL), F32), jax.ShapeDtypeStruct((t, LANES), F32),
                   jax.ShapeDtypeStruct((SUBLANES, t), jnp.int32),
                   jax.ShapeDtypeStruct((N_EXPERTS, LANES), F32)],
        scratch_shapes=[pltpu.VMEM((N_EXPERTS, 1), F32), pltpu.VMEM((4, tm, D_GROUP), BF16)],
        compiler_params=_cp(("arbitrary",)),
        name="outproj_ln_router",
    )(*ys_prompt, *ys_sample, x, w_out, g, b, wr, br)


def _moe_kernel(te_ref, nu_ref, x_ref, w1_ref, w3_ref, w2_ref, y_ref, w1b, w3b, w2b):
    i = pl.program_id(0)
    n_used = nu_ref[0]

    @pl.when(i < n_used)
    def _():
        @pl.when((i == 0) | (te_ref[i] != te_ref[jnp.maximum(i - 1, 0)]))
        def _():
            w1b[...] = w1_ref[0].astype(BF16)
            w3b[...] = w3_ref[0].astype(BF16)
            w2b[...] = w2_ref[0].astype(BF16)

        x = x_ref[...].astype(BF16)
        h1 = jnp.dot(x, w1b[...], preferred_element_type=F32)
        h3 = jnp.dot(x, w3b[...], preferred_element_type=F32)
        hh = (_silu(h1) * h3).astype(BF16)
        y_ref[...] = jnp.dot(hh, w2b[...], preferred_element_type=F32)

    @pl.when(i >= n_used)
    def _():
        y_ref[...] = jnp.zeros(y_ref.shape, F32)


def _moe(tile_expert, n_used, x_sorted, w1, w3, w2, *, n_tiles, layer):
    tm = MOE_TM
    xmap = lambda i, te, nu: (jnp.maximum(jnp.minimum(i, nu[0] - 1), 0), 0)
    wmap = lambda i, te, nu: (layer * N_EXPERTS + te[i], 0, 0)
    grid_spec = pltpu.PrefetchScalarGridSpec(
        num_scalar_prefetch=2,
        grid=(n_tiles,),
        in_specs=[pl.BlockSpec((tm, D_MODEL), xmap),
                  pl.BlockSpec((1, D_MODEL, D_EXPERT), wmap),
                  pl.BlockSpec((1, D_MODEL, D_EXPERT), wmap),
                  pl.BlockSpec((1, D_EXPERT, D_MODEL), wmap)],
        out_specs=pl.BlockSpec((tm, D_MODEL), lambda i, te, nu: (i, 0)),
        scratch_shapes=[pltpu.VMEM((D_MODEL, D_EXPERT), BF16), pltpu.VMEM((D_MODEL, D_EXPERT), BF16),
                        pltpu.VMEM((D_EXPERT, D_MODEL), BF16)],
    )
    return pl.pallas_call(
        _moe_kernel,
        grid_spec=grid_spec,
        out_shape=jax.ShapeDtypeStruct((n_tiles * tm, D_MODEL), F32),
        compiler_params=_cp(("arbitrary",)),
        name="moe_grouped",
    )(tile_expert, n_used, x_sorted, w1, w3, w2)


def _disp_row_copy(x_ref, xs_hbm, sem, r, row):
    return pltpu.make_async_copy(x_ref.at[pl.ds(r, 1), :], xs_hbm.at[pl.ds(row, 1), :], sem)


def _disp_tile_copy(zbuf, xs_hbm, sem, tile, tm):
    return pltpu.make_async_copy(zbuf, xs_hbm.at[pl.ds(pl.multiple_of(tile * tm, tm), tm), :], sem)


def _dispatch_kernel(pos_ref, plo_ref, phi_ref, nu_ref, x_ref, xs_hbm, zbuf, sem, zsem, tsem, *, tm, t, mt, n_tiles):
    i = pl.program_id(0)

    def body(r, c):
        for k in range(2):
            _disp_row_copy(x_ref, xs_hbm, sem, r, pos_ref[k * t + i * tm + r]).start()
        return c
    lax.fori_loop(0, tm, body, 0, unroll=DMA_UNROLL)

    @pl.when(i == 0)
    def _():
        zbuf[...] = jnp.zeros(zbuf.shape, F32)
        for e in range(N_EXPERTS):
            def zbody(row, c):
                _disp_row_copy(zbuf, xs_hbm, zsem, 0, row).start()
                return c
            lax.fori_loop(plo_ref[e], phi_ref[e], zbody, 0)

        def tbody(tile, c):
            _disp_tile_copy(zbuf, xs_hbm, tsem, tile, mt).start()
            return c
        lax.fori_loop(nu_ref[0], n_tiles, tbody, 0)
        for e in range(N_EXPERTS):
            def zwait(row, c):
                _disp_row_copy(zbuf, xs_hbm, zsem, 0, row).wait()
                return c
            lax.fori_loop(plo_ref[e], phi_ref[e], zwait, 0)

        def twait(tile, c):
            _disp_tile_copy(zbuf, xs_hbm, tsem, tile, mt).wait()
            return c
        lax.fori_loop(nu_ref[0], n_tiles, twait, 0)

    def wbody(r, c):
        for k in range(2):
            _disp_row_copy(x_ref, xs_hbm, sem, r, 0).wait()
        return c
    lax.fori_loop(0, tm, wbody, 0, unroll=DMA_UNROLL)


def _dispatch(pos, pad_lo, pad_hi, n_used, x1, *, n_tiles):
    t = x1.shape[0]
    tm = _pick(t, (640, 128))
    mt = MOE_TM
    kern = functools.partial(_dispatch_kernel, tm=tm, t=t, mt=mt, n_tiles=n_tiles)
    grid_spec = pltpu.PrefetchScalarGridSpec(
        num_scalar_prefetch=4,
        grid=(t // tm,),
        in_specs=[pl.BlockSpec((tm, D_MODEL), lambda i, *_: (i, 0))],
        out_specs=pl.BlockSpec(memory_space=pl.ANY),
        scratch_shapes=[pltpu.VMEM((mt, D_MODEL), F32), pltpu.SemaphoreType.DMA(()),
                        pltpu.SemaphoreType.DMA(()), pltpu.SemaphoreType.DMA(())],
    )
    return pl.pallas_call(
        kern,
        grid_spec=grid_spec,
        out_shape=jax.ShapeDtypeStruct((n_tiles * mt, D_MODEL), F32),
        compiler_params=_cp(("arbitrary",)),
        name="moe_dispatch",
    )(pos, pad_lo, pad_hi, n_used, x1)


def _moe_plan(ids, cnt, n_tiles):
    tm = MOE_TM
    counts = cnt[:, 0].astype(jnp.int32)
    ntile_e = (counts + tm - 1) // tm
    tile_end = jnp.cumsum(ntile_e)
    row_start = (tile_end - ntile_e) * tm
    n_used = tile_end[-1]
    tiles = jnp.arange(n_tiles, dtype=jnp.int32)
    te = jnp.sum((tiles[:, None] >= tile_end[None, :]).astype(jnp.int32), 1)
    last_e = jnp.sum((n_used - 1 >= tile_end).astype(jnp.int32))
    te = jnp.where(tiles < n_used, te, last_e).astype(jnp.int32)
    eid, rank = ids[0:2], ids[2:4]
    sel = eid[:, :, None] == jnp.arange(N_EXPERTS, dtype=jnp.int32)
    pos = (jnp.sum(jnp.where(sel, row_start, 0), -1) + rank).reshape(-1).astype(jnp.int32)
    pad_lo = (row_start + counts).astype(jnp.int32)
    pad_hi = (row_start + ntile_e * tm).astype(jnp.int32)
    return te, n_used.reshape(1).astype(jnp.int32), pos, pad_lo, pad_hi


def _comb_row_copy(y_hbm, ybuf, sem, row, slot, k, r):
    return pltpu.make_async_copy(y_hbm.at[pl.ds(row, 1), :], ybuf.at[slot, k, pl.ds(r, 1), :], sem.at[slot])


def _combine_kernel(pos_ref, y_hbm, x1_ref, gc_ref, g_ref, b_ref, xf_ref, xb_ref, ybuf, sem, *, tm, t):
    i = pl.program_id(0)
    n = pl.num_programs(0)

    def start_tile(tile, slot):
        def body(r, c):
            for k in range(2):
                _comb_row_copy(y_hbm, ybuf, sem, pos_ref[k * t + tile * tm + r], slot, k, r).start()
            return c
        lax.fori_loop(0, tm, body, 0, unroll=DMA_UNROLL)

    @pl.when(i == 0)
    def _():
        start_tile(0, 0)

    @pl.when(i + 1 < n)
    def _():
        start_tile(i + 1, (i + 1) % 2)

    slot = i % 2

    def wbody(r, c):
        for k in range(2):
            _comb_row_copy(y_hbm, ybuf, sem, 0, slot, k, r).wait()
        return c
    lax.fori_loop(0, tm, wbody, 0, unroll=DMA_UNROLL)
    ffn = gc_ref[:, 0:1] * ybuf[slot, 0] + gc_ref[:, 1:2] * ybuf[slot, 1]
    x2 = _ln(ALPHA * x1_ref[...] + ffn, g_ref[...], b_ref[...])
    xf_ref[...] = x2
    xb_ref[...] = x2.astype(BF16)


def _combine(pos, y_sorted, x1, gcol, g, b):
    t = x1.shape[0]
    tm = _pick(t, (320, 128))
    kern = functools.partial(_combine_kernel, tm=tm, t=t)
    grid_spec = pltpu.PrefetchScalarGridSpec(
        num_scalar_prefetch=1,
        grid=(t // tm,),
        in_specs=[pl.BlockSpec(memory_space=pl.ANY),
                  pl.BlockSpec((tm, D_MODEL), lambda i, pos: (i, 0)),
                  pl.BlockSpec((tm, LANES), lambda i, pos: (i, 0)),
                  pl.BlockSpec((1, D_MODEL), lambda i, pos: (0, 0)),
                  pl.BlockSpec((1, D_MODEL), lambda i, pos: (0, 0))],
        out_specs=[pl.BlockSpec((tm, D_MODEL), lambda i, pos: (i, 0)),
                   pl.BlockSpec((tm, D_MODEL), lambda i, pos: (i, 0))],
        scratch_shapes=[pltpu.VMEM((2, 2, tm, D_MODEL), F32), pltpu.SemaphoreType.DMA((2,))],
    )
    return pl.pallas_call(
        kern,
        grid_spec=grid_spec,
        out_shape=[jax.ShapeDtypeStruct((t, D_MODEL), F32), jax.ShapeDtypeStruct((t, D_MODEL), BF16)],
        compiler_params=_cp(("arbitrary",)),
        name="moe_combine_ln",
    )(pos, y_sorted, x1, gcol, g, b)


def _row(v, n=None):
    v = v.reshape(1, -1).astype(F32)
    if n is not None and v.shape[1] < n:
        v = jnp.pad(v, ((0, 0), (0, n - v.shape[1])))
    return v


def kernel(x_prompt, x_sample, state_ssm_conv, state_ssm, cache_k, cache_v, state_conv, ln_in_g, ln_in_b, w_in, ssm_conv_w, ssm_conv_b, ssm_dt_bias, ssm_a_log, ssm_d, ssm_norm_g, da_lq1, da_lk1, da_lq2, da_lk2, da_norm_g, sg_ln_g, sg_ln_b, sg_w, sg_b, cv_w, cv_b, cv_ln_g, cv_ln_b, w_out, ln1_g, ln1_b, moe_wg_group, moe_bg_group, moe_wg_exp, moe_bg_exp, moe_w1, moe_w3, moe_w2, ln2_g, ln2_b):
    bp, tp, _ = x_prompt.shape
    nb, ls, _ = x_sample.shape
    depth = w_in.shape[0]
    past = cache_k.shape[2]
    assert bp == 1 and tp % 512 == 0 and ls % SUBLANES == 0 and ls <= CHUNK
    ts = nb * ls
    t = tp + ts
    n_tiles = (2 * t + N_EXPERTS * (MOE_TM - 1)) // MOE_TM + 1

    dt0 = D_GROUP + SSM_CONV_CH
    w1s = moe_w1.reshape(depth * N_EXPERTS, D_MODEL, D_EXPERT)
    w3s = moe_w3.reshape(depth * N_EXPERTS, D_MODEL, D_EXPERT)
    w2s = moe_w2.reshape(depth * N_EXPERTS, D_EXPERT, D_MODEL)
    wr = jnp.zeros((depth, R_ROWS, D_MODEL), F32)
    wr = wr.at[:, 0:N_EGROUPS].set(jnp.swapaxes(moe_wg_group, 1, 2))
    wr = wr.at[:, SUBLANES:SUBLANES + N_EXPERTS].set(jnp.swapaxes(moe_wg_exp, 1, 2))
    br = jnp.zeros((depth, R_ROWS, 1), F32)
    br = br.at[:, 0:N_EGROUPS, 0].set(moe_bg_group).at[:, SUBLANES:SUBLANES + N_EXPERTS, 0].set(moe_bg_exp)
    cache_k2 = cache_k.reshape(depth * nb, past, D_GROUP)
    cache_v2 = cache_v.reshape(depth * nb, past, D_GROUP)

    pos_all = jnp.concatenate([jnp.arange(tp), jnp.tile(past + jnp.arange(ls), nb)])
    cos_t, sin1_t, sin2_t = _rope_tables(pos_all)

    x_all = jnp.concatenate([x_prompt.reshape(tp, D_MODEL), x_sample.reshape(ts, D_MODEL)], 0)
    xf, xb = _ln_in(x_all, ln_in_g, ln_in_b)

    zeros_hist = jnp.zeros((1, SUBLANES, SSM_CONV_CH), F32)
    zeros_state = jnp.zeros((1, SSM_STATE, D_GROUP), F32)
    zeros_cv = jnp.zeros((1, CV_HIST, D_GROUP), F32)
    outs = {k: [] for k in ("p_sc", "p_ss", "p_k", "p_v", "p_cv", "s_sc", "s_ss", "s_k", "s_v", "s_cv", "s_sg")}
    tl_p = 512
    tl_c = 256

    for l in range(depth):
        lam_init = 0.8 - 0.6 * math.exp(-0.3 * l)
        w_in_l = w_in[l]
        w_main = jnp.concatenate([w_in_l[:, :dt0], w_in_l[:, dt0 + SSM_HEADS:]], -1).astype(BF16)
        w_dt = jnp.pad(w_in_l[:, dt0:dt0 + SSM_HEADS], ((0, 0), (0, LANES - SSM_HEADS))).astype(BF16)
        w_out_b = w_out[l].reshape(4, D_GROUP, D_MODEL).astype(BF16)
        h, dt = _inproj(xb, w_main, w_dt)

        ssd_p = dict(cw=ssm_conv_w[l], cb=_row(ssm_conv_b[l]), dtb=_row(ssm_dt_bias[l], LANES),
                     a=_row(-jnp.exp(ssm_a_log[l].astype(F32)), LANES),
                     dsk=_row(jnp.repeat(ssm_d[l], SSM_HEAD_DIM)), ng=_row(ssm_norm_g[l]))
        ya, hist_p, st_p = _ssd(h, dt, zeros_hist, zeros_state, dict(ssd_p, len=tp, name="ssd_prompt"),
                                nb=1, tl=tl_p, q=CHUNK, row0=0, out_rows=tp)
        hist_s_in = jnp.pad(state_ssm_conv[l], ((0, 0), (SUBLANES - (SSM_CONV - 1), 0), (0, 0)))
        h0t_s = jnp.swapaxes(state_ssm[l].reshape(nb, D_GROUP, SSM_STATE), 1, 2)
        ya_s, hist_s, st_s = _ssd(h, dt, hist_s_in, h0t_s, dict(ssd_p, len=ls, name="ssd_sample"),
                                  nb=nb, tl=ls, q=ls, row0=tp, out_rows=ts)
        outs["p_sc"].append(hist_p[:, SUBLANES - (SSM_CONV - 1):])
        outs["s_sc"].append(hist_s[:, SUBLANES - (SSM_CONV - 1):])
        unt = lambda s: jnp.swapaxes(s, 1, 2).reshape(-1, SSM_HEADS, SSM_HEAD_DIM, SSM_STATE)
        outs["p_ss"].append(unt(st_p))
        outs["s_ss"].append(unt(st_s))

        qb, kf, kb, vb = _rope(h, cos_t, sin1_t, sin2_t)
        f32 = lambda a: a.astype(F32)
        lam = (jnp.exp(jnp.sum(f32(da_lq1[l]) * f32(da_lk1[l]))) - jnp.exp(jnp.sum(f32(da_lq2[l]) * f32(da_lk2[l])))
               + lam_init).reshape(1).astype(F32)
        gda = _row(da_norm_g[l])
        yb = _attn_prompt(lam, qb, kb, vb, gda, tp=tp, out_rows=tp, scale_out=1.0 - lam_init)
        yb_s = _attn_sample(lam, qb, kb, vb, cache_k2, cache_v2, gda, layer=l, nb=nb, ls=ls, row0=tp,
                            scale_out=1.0 - lam_init)
        outs["p_k"].append(kf[:tp].reshape(1, tp, DA_HEADS, 2, DA_DIM))
        outs["s_k"].append(kf[tp:].reshape(nb, ls, DA_HEADS, 2, DA_DIM))
        vf = h[:, CB_V * D_GROUP:(CB_V + 1) * D_GROUP]
        outs["p_v"].append(vf[:tp].reshape(1, tp, DA_HEADS, DA_VDIM))
        outs["s_v"].append(vf[tp:].reshape(nb, ls, DA_HEADS, DA_VDIM))

        sg_common = dict(g=_row(sg_ln_g[l]), b=_row(sg_ln_b[l]))
        bias_full = lambda qq: jnp.repeat(sg_b[l][:, :qq].T, LANES, axis=1).astype(F32)
        (yc,) = _sgu(h, dict(sg_common, w=sg_w[l], bias=bias_full(SG_CHUNK), name="sgu_prompt"),
                     nrows=tp, tl=tl_p, q=SG_CHUNK, row0=0, out_rows=tp, want_v=False)
        yc_s, v_rows = _sgu(h, dict(sg_common, w=sg_w[l][:, :ls, :ls], bias=bias_full(ls), name="sgu_sample"),
                            nrows=ts, tl=ls, q=ls, row0=tp, out_rows=ts, want_v=True)
        outs["s_sg"].append(v_rows.reshape(nb, ls, D_GROUP))

        cv_p = dict(w=jnp.pad(cv_w[l], ((0, CV_HIST - CV_WIDTH), (0, 0))), b=_row(cv_b[l]),
                    lg=_row(cv_ln_g[l]), lb=_row(cv_ln_b[l]))
        yd, tail_p = _cconv(h, zeros_cv, dict(cv_p, name="cconv_prompt"), nb=1, seqlen=tp, tl=tl_c, row0=0,
                            out_rows=tp)
        cv_hist_s = jnp.pad(state_conv[l], ((0, 0), (CV_HIST - (CV_WIDTH - 1), 0), (0, 0)))
        yd_s, tail_s = _cconv(h, cv_hist_s, dict(cv_p, name="cconv_sample"), nb=nb, seqlen=ls, tl=ls, row0=tp,
                              out_rows=ts)
        outs["p_cv"].append(tail_p[:, CV_HIST - (CV_WIDTH - 1):])
        outs["s_cv"].append(tail_s[:, CV_HIST - (CV_WIDTH - 1):])

        x1, gcol, ids, cnt = _outproj((ya, yb, yc, yd), (ya_s, yb_s, yc_s, yd_s), xf, w_out_b, _row(ln1_g[l]), _row(ln1_b[l]), wr[l], br[l])
        te, n_used, pos, pad_lo, pad_hi = _moe_plan(ids, cnt, n_tiles)
        x_sorted = _dispatch(pos, pad_lo, pad_hi, n_used, x1, n_tiles=n_tiles)
        y_sorted = _moe(te, n_used, x_sorted, w1s, w3s, w2s, n_tiles=n_tiles, layer=l)
        xf, xb = _combine(pos, y_sorted, x1, gcol, _row(ln2_g[l]), _row(ln2_b[l]))

    st = lambda k: jnp.stack(outs[k])
    return (xf[:tp].reshape(1, tp, D_MODEL), xf[tp:].reshape(nb, ls, D_MODEL),
            st("p_sc"), st("p_ss"), st("p_k"), st("p_v"), st("p_cv"),
            st("s_sc"), st("s_ss"), st("s_k"), st("s_v"), st("s_cv"), st("s_sg"))
```

```python
import functools
import math

import jax
import jax.numpy as jnp
import numpy as np
from jax import lax
from jax.experimental import pallas as pl
from jax.experimental.pallas import tpu as pltpu

F32 = jnp.float32
BF16 = jnp.bfloat16
HI = lax.Precision.HIGHEST

D_MODEL = 2048
DEPTH = 4
CHUNK = 64
D_GROUP = 512
SSM_HEADS = 8
SSM_HEAD_DIM = 64
SSM_GROUPS = 2
SSM_STATE = 128
SSM_CONV = 4
SSM_CONV_CH = 1024
DA_HEADS = 4
DA_DIM = 64
DA_VDIM = 128
ROPE_DIM = 16
ROPE_THETA = 500000.0
SG_CHUNK = 128
SG_GROUPS = 4
CV_WIDTH = 31
N_EGROUPS = 4
EXP_PER_GROUP = 4
N_EXPERTS = 16
D_EXPERT = 512
ALPHA = (2 * DEPTH) ** 0.25
EPS = 1e-5

LANES = 128
SUBLANES = 8
VMEM_LIMIT = 56 * 1024 * 1024
D_MAIN = 10 * D_GROUP
MOE_TM = 256
DMA_UNROLL = 8
NT = (((1,), (1,)), ((), ()))
TN = (((0,), (0,)), ((), ()))

CB_Z, CB_XS, CB_BC, CB_Q, CB_K, CB_V, CB_SU, CB_SV, CB_CA, CB_CG = range(10)


def _cp(sem):
    return pltpu.CompilerParams(dimension_semantics=sem, vmem_limit_bytes=VMEM_LIMIT)


def _pick(n, cands):
    for c in cands:
        if n % c == 0:
            return c
    raise ValueError(f"no tile for {n}")


def _ln(x, g, b):
    mu = jnp.mean(x, -1, keepdims=True)
    xc = x - mu
    var = jnp.mean(xc * xc, -1, keepdims=True)
    return xc * lax.rsqrt(var + EPS) * g + b


def _silu(x):
    return x * jax.nn.sigmoid(x)


def _softplus(x):
    return jnp.maximum(x, 0.0) + jnp.log1p(jnp.exp(-jnp.abs(x)))


def _ln_in_kernel(xp_ref, xs_ref, g_ref, b_ref, xf_ref, xb_ref, *, n_full, rem):
    i = pl.program_id(0)
    tm = xf_ref.shape[0]

    def emit(lo, hi, x):
        y = _ln(x, g_ref[...], b_ref[...])
        xf_ref[lo:hi, :] = y
        xb_ref[lo:hi, :] = y.astype(BF16)

    @pl.when(i < n_full)
    def _():
        emit(0, tm, xp_ref[...])

    @pl.when(i >= n_full)
    def _():
        if rem:
            emit(0, rem, xp_ref[0:rem, :])
        emit(rem, tm, xs_ref[...])


def _ln_in(xp, xs, g, b):
    tp, ts = xp.shape[0], xs.shape[0]
    t = tp + ts
    tm = _pick(t, (640, 128))
    n_full, rem = divmod(tp, tm)
    assert rem + ts == tm, "sample rows must complete the last token tile"
    last_pb = pl.cdiv(tp, tm) - 1
    kern = functools.partial(_ln_in_kernel, n_full=n_full, rem=rem)
    return pl.pallas_call(
        kern,
        grid=(t // tm,),
        in_specs=[pl.BlockSpec((tm, D_MODEL), lambda i: (jnp.minimum(i, last_pb), 0)),
                  pl.BlockSpec((ts, D_MODEL), lambda i: (0, 0)),
                  pl.BlockSpec((1, D_MODEL), lambda i: (0, 0)),
                  pl.BlockSpec((1, D_MODEL), lambda i: (0, 0))],
        out_specs=[pl.BlockSpec((tm, D_MODEL), lambda i: (i, 0)),
                   pl.BlockSpec((tm, D_MODEL), lambda i: (i, 0))],
        out_shape=[jax.ShapeDtypeStruct((t, D_MODEL), F32), jax.ShapeDtypeStruct((t, D_MODEL), BF16)],
        compiler_params=_cp(("parallel",)),
        name="ln_in",
    )(xp, xs, g.reshape(1, -1), b.reshape(1, -1))


def _inproj_kernel(x_ref, w_ref, wdt_ref, h_ref, dt_ref):
    x = x_ref[...]
    h_ref[...] = jnp.dot(x, w_ref[...], preferred_element_type=F32)

    @pl.when(pl.program_id(1) == 0)
    def _():
        dt_ref[...] = jnp.dot(x, wdt_ref[...], preferred_element_type=F32)


def _inproj(xb, w_main, w_dt):
    t = xb.shape[0]
    tm = _pick(t, (1664, 1024, 128))
    tn = D_GROUP
    return pl.pallas_call(
        _inproj_kernel,
        grid=(t // tm, D_MAIN // tn),
        in_specs=[pl.BlockSpec((tm, D_MODEL), lambda i, j: (i, 0)),
                  pl.BlockSpec((D_MODEL, tn), lambda i, j: (0, j)),
                  pl.BlockSpec((D_MODEL, LANES), lambda i, j: (0, 0))],
        out_specs=[pl.BlockSpec((tm, tn), lambda i, j: (i, j)),
                   pl.BlockSpec((tm, LANES), lambda i, j: (i, 0))],
        out_shape=[jax.ShapeDtypeStruct((t, D_MAIN), F32), jax.ShapeDtypeStruct((t, LANES), F32)],
        compiler_params=_cp(("parallel", "arbitrary")),
        name="inproj",
    )(xb, w_main, w_dt)


def _rope_kernel(q_ref, k_ref, v_ref, c_ref, s1_ref, s2_ref, qb_ref, kf_ref, kb_ref, vb_ref, vf_ref):
    c = c_ref[...]
    s1 = s1_ref[...]
    s2 = s2_ref[...]

    def rot(x):
        return x * c + pltpu.roll(x, LANES - ROPE_DIM // 2, 1) * s1 + pltpu.roll(x, ROPE_DIM // 2, 1) * s2

    for h in range(DA_HEADS):
        sl = slice(h * LANES, (h + 1) * LANES)
        qr = rot(q_ref[:, sl])
        kr = rot(k_ref[:, sl])
        qb_ref[:, sl] = (qr * (DA_DIM ** -0.5)).astype(BF16)
        kf_ref[:, sl] = kr
        kb_ref[:, sl] = kr.astype(BF16)
    v = v_ref[...]
    vb_ref[...] = v.astype(BF16)
    vf_ref[...] = v


def _rope(h, tables, *, row0, nrows):
    tm = _pick(nrows, (512, 128))
    rb0 = row0 // tm
    hspec = lambda cb: pl.BlockSpec((tm, D_GROUP), lambda i, cb=cb: (rb0 + i, cb))
    tspec = pl.BlockSpec((tm, LANES), lambda i: (i, 0))
    ospec = pl.BlockSpec((tm, D_GROUP), lambda i: (i, 0))
    return pl.pallas_call(
        _rope_kernel,
        grid=(nrows // tm,),
        in_specs=[hspec(CB_Q), hspec(CB_K), hspec(CB_V), tspec, tspec, tspec],
        out_specs=[ospec, ospec, ospec, ospec, ospec],
        out_shape=[jax.ShapeDtypeStruct((nrows, D_GROUP), BF16), jax.ShapeDtypeStruct((nrows, D_GROUP), F32),
                   jax.ShapeDtypeStruct((nrows, D_GROUP), BF16), jax.ShapeDtypeStruct((nrows, D_GROUP), BF16),
                   jax.ShapeDtypeStruct((nrows, D_GROUP), F32)],
        compiler_params=_cp(("parallel",)),
        name="rope",
    )(h, h, h, *tables)


def _rope_tables(pos):
    half = ROPE_DIM // 2
    inv_freq = 1.0 / (ROPE_THETA ** (np.arange(half, dtype=np.float64) * 2.0 / ROPE_DIM))
    ang = np.asarray(pos, np.float64)[:, None] * inv_freq
    cos, sin = np.cos(ang), np.sin(ang)
    n = ang.shape[0]
    one = np.ones((n, DA_DIM - ROPE_DIM))
    zero = np.zeros((n, DA_DIM - ROPE_DIM))
    z8 = np.zeros((n, half))
    c = np.concatenate([cos, cos, one], 1)
    s1 = np.concatenate([-sin, z8, zero], 1)
    s2 = np.concatenate([z8, sin, zero], 1)
    return tuple(jnp.asarray(np.concatenate([a, a], 1), F32) for a in (c, s1, s2))


def _ssd_kernel(z_ref, xs_ref, bc_ref, dt_ref, hist_ref, h0_ref, cw_ref, cb_ref, dtb_ref, a_ref, dsk_ref, ng_ref,
                y_ref, hist_o_ref, st_o_ref, xbuf, act, dts, ht, *, tl, q):
    hp = SSM_HEADS // SSM_GROUPS
    gw = hp * SSM_HEAD_DIM

    @pl.when(pl.program_id(1) == 0)
    def _():
        xbuf[0:SUBLANES, :] = hist_ref[0]
        ht[...] = h0_ref[0]

    xbuf[SUBLANES:SUBLANES + tl, 0:D_GROUP] = xs_ref[...]
    xbuf[SUBLANES:SUBLANES + tl, D_GROUP:2 * D_GROUP] = bc_ref[...]
    acc = cb_ref[...] + cw_ref[SSM_CONV - 1:SSM_CONV, :] * xbuf[SUBLANES:SUBLANES + tl, :]
    for k in range(SSM_CONV - 1):
        off = SUBLANES - (SSM_CONV - 1) + k
        acc = acc + cw_ref[k:k + 1, :] * xbuf[off:off + tl, :]
    act[...] = _silu(acc)
    tail = xbuf[tl:tl + SUBLANES, :]
    xbuf[0:SUBLANES, :] = tail
    hist_o_ref[0] = tail
    dts[...] = _softplus(dt_ref[...] + dtb_ref[...])

    ri = lax.broadcasted_iota(jnp.int32, (q, q), 0)
    ci = lax.broadcasted_iota(jnp.int32, (q, q), 1)
    causal = ri >= ci
    tril = causal.astype(F32)
    expand = (lax.broadcasted_iota(jnp.int32, (LANES, D_GROUP), 1) // SSM_HEAD_DIM
              == lax.broadcasted_iota(jnp.int32, (LANES, D_GROUP), 0)).astype(F32)
    eye8 = (lax.broadcasted_iota(jnp.int32, (SUBLANES, LANES), 0)
            == lax.broadcasted_iota(jnp.int32, (SUBLANES, LANES), 1)).astype(F32)
    lane_head = lax.broadcasted_iota(jnp.int32, (q, gw), 1) // SSM_HEAD_DIM

    def chunk(ci_, carry):
        r0 = pl.multiple_of(ci_ * q, q)
        dt_c = dts[pl.ds(r0, q), :]
        da = dt_c * a_ref[...]
        cs = jnp.dot(tril, da, precision=HI, preferred_element_type=F32)
        csx = jnp.dot(cs, expand, precision=HI, preferred_element_type=F32)
        dtx = jnp.dot(dt_c, expand, precision=HI, preferred_element_type=F32)
        cst = lax.dot_general(eye8, cs, NT, precision=HI, preferred_element_type=F32)
        dtt = lax.dot_general(eye8, dt_c, NT, precision=HI, preferred_element_type=F32)
        xs = act[pl.ds(r0, q), 0:D_GROUP]
        ecs = jnp.exp(csx)
        cs_last = csx[q - 1:q, :]
        xw = (xs * (jnp.exp(cs_last - csx) * dtx)).astype(BF16)
        xsb = xs.astype(BF16)
        dec = jnp.exp(cs_last)
        ys = []
        for g in range(SSM_GROUPS):
            gl = slice(g * gw, (g + 1) * gw)
            bg = act[pl.ds(r0, q), D_GROUP + g * SSM_STATE:D_GROUP + (g + 1) * SSM_STATE].astype(BF16)
            c0 = D_GROUP + SSM_GROUPS * SSM_STATE
            cg = act[pl.ds(r0, q), c0 + g * SSM_STATE:c0 + (g + 1) * SSM_STATE].astype(BF16)
            cb = lax.dot_general(cg, bg, NT, preferred_element_type=F32)
            hin = ht[:, gl]
            yg = jnp.dot(cg, hin.astype(BF16), preferred_element_type=F32) * ecs[:, gl]
            xg = xsb[:, gl]
            for hh in range(hp):
                h = g * hp + hh
                seg = cs[:, h:h + 1] - cst[h:h + 1, :]
                lm = jnp.exp(jnp.where(causal, seg, -jnp.inf))
                m = (cb * lm * dtt[h:h + 1, :]).astype(BF16)
                xm = jnp.where(lane_head == hh, xg, jnp.zeros_like(xg))
                yg = yg + jnp.dot(m, xm, preferred_element_type=F32)
            st = lax.dot_general(bg, xw[:, gl], TN, preferred_element_type=F32)
            ht[:, gl] = dec[:, gl] * hin + st
            ys.append(yg)
        y = jnp.concatenate(ys, axis=1) + dsk_ref[...] * xs
        y = y * _silu(z_ref[pl.ds(r0, q), :])
        outs = []
        for g in range(SSM_GROUPS):
            yg = y[:, g * gw:(g + 1) * gw]
            outs.append(yg * lax.rsqrt(jnp.mean(yg * yg, -1, keepdims=True) + EPS))
        y = jnp.concatenate(outs, axis=1) * ng_ref[...]
        y_ref[pl.ds(r0, q), :] = y.astype(y_ref.dtype)
        return carry

    lax.fori_loop(0, tl // q, chunk, 0)
    st_o_ref[0] = ht[...]


def _ssd(h, dt, hist, h0t, p, *, nb, tl, q, row0, out_rows, orow0=0):
    rb0 = row0 // tl
    ob0 = orow0 // tl
    nt = (p["len"]) // tl

    def hspec(cb):
        return pl.BlockSpec((tl, D_GROUP), lambda b, t, cb=cb: (rb0 + b * nt + t, cb))

    full = lambda shape: pl.BlockSpec(shape, lambda b, t: (0,) * len(shape))
    kern = functools.partial(_ssd_kernel, tl=tl, q=q)
    return pl.pallas_call(
        kern,
        grid=(nb, nt),
        in_specs=[hspec(CB_Z), hspec(CB_XS), hspec(CB_BC),
                  pl.BlockSpec((tl, LANES), lambda b, t: (rb0 + b * nt + t, 0)),
                  pl.BlockSpec((1, SUBLANES, SSM_CONV_CH), lambda b, t: (b, 0, 0)),
                  pl.BlockSpec((1, SSM_STATE, D_GROUP), lambda b, t: (b, 0, 0)),
                  full((SSM_CONV, SSM_CONV_CH)), full((1, SSM_CONV_CH)), full((1, LANES)), full((1, LANES)),
                  full((1, D_GROUP)), full((1, D_GROUP))],
        out_specs=[pl.BlockSpec((tl, D_GROUP), lambda b, t: (ob0 + b * nt + t, 0)),
                   pl.BlockSpec((1, SUBLANES, SSM_CONV_CH), lambda b, t: (b, 0, 0)),
                   pl.BlockSpec((1, SSM_STATE, D_GROUP), lambda b, t: (b, 0, 0))],
        out_shape=[jax.ShapeDtypeStruct((out_rows, D_GROUP), BF16),
                   jax.ShapeDtypeStruct((nb, SUBLANES, SSM_CONV_CH), F32),
                   jax.ShapeDtypeStruct((nb, SSM_STATE, D_GROUP), F32)],
        scratch_shapes=[pltpu.VMEM((tl + SUBLANES, SSM_CONV_CH), F32), pltpu.VMEM((tl, SSM_CONV_CH), F32),
                        pltpu.VMEM((tl, LANES), F32), pltpu.VMEM((SSM_STATE, D_GROUP), F32)],
        compiler_params=_cp(("parallel", "arbitrary")),
        name=p["name"],
    )(h, h, h, dt, hist, h0t, p["cw"], p["cb"], p["dtb"], p["a"], p["dsk"], p["ng"])


def _lane_fold(x, op):
    parts = [x[:, t * LANES:(t + 1) * LANES] for t in range(x.shape[1] // LANES)]
    while len(parts) > 1:
        parts = [op(parts[a], parts[a + 1]) for a in range(0, len(parts), 2)]
    return parts[0]


def _attn_kernel(it_ref, jt_ref, pt_ref, lam_ref, q_ref, k_ref, v_ref, g_ref, o_ref,
                 mx_ref, mrep_ref, ls_ref, acc_ref, *, tq, scale_out):
    step = pl.program_id(1)
    i = it_ref[step]
    j = jt_ref[step]
    sweep = pt_ref[step]
    nl = tq // LANES

    def scores(c, diagonal):
        sl = slice(c * DA_DIM, (c + 1) * DA_DIM)
        s = lax.dot_general(q_ref[:, sl], k_ref[:, sl], NT, preferred_element_type=F32)
        if diagonal:
            rowc = lax.broadcasted_iota(jnp.int32, (tq, tq), 0) // CHUNK
            colc = lax.broadcasted_iota(jnp.int32, (tq, tq), 1) // CHUNK
            s = jnp.where(colc <= rowc, s, -jnp.inf)
        return s

    def sweep0(diagonal):
        for c in range(2):
            mx_ref[c] = jnp.maximum(mx_ref[c], _lane_fold(scores(c, diagonal), jnp.maximum))
        if diagonal:
            for c in range(2):
                m = jnp.max(mx_ref[c], -1, keepdims=True)
                mrep_ref[c] = jnp.broadcast_to(m, (tq, LANES))
            ls_ref[...] = jnp.zeros(ls_ref.shape, F32)
            acc_ref[...] = jnp.zeros(acc_ref.shape, F32)

    def sweep1(diagonal):
        v = v_ref[...]
        for c in range(2):
            m = mrep_ref[c]
            p = jnp.exp(scores(c, diagonal) - jnp.concatenate([m] * nl, axis=1))
            ls_ref[c] = ls_ref[c] + _lane_fold(p, jnp.add)
            acc_ref[c] = acc_ref[c] + jnp.dot(p.astype(BF16), v, preferred_element_type=F32)
        if diagonal:
            l0 = jnp.sum(ls_ref[0], -1, keepdims=True)
            l1 = jnp.sum(ls_ref[1], -1, keepdims=True)
            o = acc_ref[0] / l0 - lam_ref[0] * (acc_ref[1] / l1)
            ms = jnp.mean(o * o, -1, keepdims=True)
            o_ref[...] = (o * lax.rsqrt(ms + EPS) * g_ref[...] * scale_out).astype(o_ref.dtype)

    @pl.when((sweep == 0) & (j == 0))
    def _():
        mx_ref[...] = jnp.full(mx_ref.shape, -jnp.inf, F32)

    for sw, fn in ((0, sweep0), (1, sweep1)):
        @pl.when((sweep == sw) & (j < i))
        def _(fn=fn):
            fn(False)

        @pl.when((sweep == sw) & (j == i))
        def _(fn=fn):
            fn(True)


def _attn_prompt(lam, qb, kb, vb, g, *, tp, out_rows, scale_out):
    tq = _pick(tp, (1024, 512, 256, 128))
    n = tp // tq
    steps = [(i, j, sw) for i in range(n) for sw in range(2) for j in range(i + 1)]
    it, jt, pt = (jnp.asarray([s[a] for s in steps], jnp.int32) for a in range(3))
    kern = functools.partial(_attn_kernel, tq=tq, scale_out=scale_out)
    grid_spec = pltpu.PrefetchScalarGridSpec(
        num_scalar_prefetch=3,
        grid=(DA_HEADS, len(steps)),
        in_specs=[pl.BlockSpec(memory_space=pltpu.SMEM),
                  pl.BlockSpec((tq, LANES), lambda h, s, it, jt, pt: (it[s], h)),
                  pl.BlockSpec((tq, LANES), lambda h, s, it, jt, pt: (jt[s], h)),
                  pl.BlockSpec((tq, LANES), lambda h, s, it, jt, pt: (jt[s] * pt[s], h)),
                  pl.BlockSpec((1, LANES), lambda h, s, it, jt, pt: (0, 0))],
        out_specs=pl.BlockSpec((tq, LANES), lambda h, s, it, jt, pt: (it[s], h)),
        scratch_shapes=[pltpu.VMEM((2, tq, LANES), F32), pltpu.VMEM((2, tq, LANES), F32),
                        pltpu.VMEM((2, tq, LANES), F32), pltpu.VMEM((2, tq, LANES), F32)],
    )
    return pl.pallas_call(
        kern,
        grid_spec=grid_spec,
        out_shape=jax.ShapeDtypeStruct((out_rows, D_GROUP), BF16),
        compiler_params=_cp(("parallel", "arbitrary")),
        name="attn_prompt",
    )(it, jt, pt, lam, qb, kb, vb, g)


def _attn_s_kernel(lam_ref, q_ref, kc_ref, vc_ref, kn_ref, vn_ref, g_ref, o_ref, *, scale_out):
    q = q_ref[...]
    kc = kc_ref[0].astype(BF16)
    vc = vc_ref[0].astype(BF16)
    kn = kn_ref[...]
    vn = vn_ref[...]
    outs = []
    for c in range(2):
        sl = slice(c * DA_DIM, (c + 1) * DA_DIM)
        sp = lax.dot_general(q[:, sl], kc[:, sl], NT, preferred_element_type=F32)
        sn = lax.dot_general(q[:, sl], kn[:, sl], NT, preferred_element_type=F32)
        m = jnp.maximum(jnp.max(sp, -1, keepdims=True), jnp.max(sn, -1, keepdims=True))
        pp = jnp.exp(sp - m)
        pn = jnp.exp(sn - m)
        l = jnp.sum(pp, -1, keepdims=True) + jnp.sum(pn, -1, keepdims=True)
        o = jnp.dot(pp.astype(BF16), vc, preferred_element_type=F32) + jnp.dot(pn.astype(BF16), vn,
                                                                               preferred_element_type=F32)
        outs.append(o / l)
    o = outs[0] - lam_ref[0] * outs[1]
    ms = jnp.mean(o * o, -1, keepdims=True)
    o_ref[...] = (o * lax.rsqrt(ms + EPS) * g_ref[...] * scale_out).astype(o_ref.dtype)


def _attn_sample(lam, qb, kb, vb, cache_k, cache_v, g, *, layer, nb, ls, row0, scale_out):
    past = cache_k.shape[1]
    rb0 = row0 // ls
    new = lambda: pl.BlockSpec((ls, LANES), lambda b, h: (rb0 + b, h))
    cache = lambda: pl.BlockSpec((1, past, LANES), lambda b, h: (layer * nb + b, 0, h))
    kern = functools.partial(_attn_s_kernel, scale_out=scale_out)
    return pl.pallas_call(
        kern,
        grid=(nb, DA_HEADS),
        in_specs=[pl.BlockSpec(memory_space=pltpu.SMEM),
                  new(), cache(), cache(), new(), new(), pl.BlockSpec((1, LANES), lambda b, h: (0, 0))],
        out_specs=pl.BlockSpec((ls, LANES), lambda b, h: (b, h)),
        out_shape=jax.ShapeDtypeStruct((nb * ls, D_GROUP), BF16),
        compiler_params=_cp(("parallel", "parallel")),
        name="attn_sample",
    )(lam, qb, cache_k, cache_v, kb, vb, g)


def _sgu_kernel(u_ref, v_ref, g_ref, b_ref, w_ref, bias_ref, y_ref, vo_ref, *, tl, q):
    v = _ln(v_ref[...], g_ref[...], b_ref[...])
    if vo_ref is not None:
        vo_ref[...] = v
    vb = v.astype(BF16)
    tri = lax.broadcasted_iota(jnp.int32, (q, q), 0) >= lax.broadcasted_iota(jnp.int32, (q, q), 1)
    for g in range(SG_GROUPS):
        w = jnp.where(tri, w_ref[g], 0.0).astype(BF16)
        sl = slice(g * LANES, (g + 1) * LANES)
        for c in range(tl // q):
            rs = slice(c * q, (c + 1) * q)
            s = jnp.dot(w, vb[rs, sl], preferred_element_type=F32) + bias_ref[:, sl]
            y_ref[rs, sl] = (u_ref[rs, sl] * s).astype(y_ref.dtype)


def _sgu(h, p, *, nrows, tl, q, row0, out_rows, want_v, orow0=0):
    rb0 = row0 // tl
    ob0 = orow0 // tl
    hspec = lambda cb: pl.BlockSpec((tl, D_GROUP), lambda i, cb=cb: (rb0 + i, cb))
    full = lambda shape: pl.BlockSpec(shape, lambda i: (0,) * len(shape))
    out_specs = [pl.BlockSpec((tl, D_GROUP), lambda i: (ob0 + i, 0))]
    out_shape = [jax.ShapeDtypeStruct((out_rows, D_GROUP), BF16)]
    if want_v:
        out_specs.append(pl.BlockSpec((tl, D_GROUP), lambda i: (i, 0)))
        out_shape.append(jax.ShapeDtypeStruct((nrows, D_GROUP), F32))
        kern = functools.partial(_sgu_kernel, tl=tl, q=q)
    else:
        kern = lambda u, v, g, b, w, bias, y: _sgu_kernel(u, v, g, b, w, bias, y, None, tl=tl, q=q)
    return pl.pallas_call(
        kern,
        grid=(nrows // tl,),
        in_specs=[hspec(CB_SU), hspec(CB_SV), full((1, D_GROUP)), full((1, D_GROUP)),
                  full((SG_GROUPS, q, q)), full((q, D_GROUP))],
        out_specs=out_specs,
        out_shape=out_shape,
        compiler_params=_cp(("parallel",)),
        name=p["name"],
    )(h, h, p["g"], p["b"], p["w"], p["bias"])


CV_HIST = 32
CV_RB = 64


def _cconv_kernel(a_ref, gt_ref, hist_ref, w_ref, b_ref, lg_ref, lb_ref, y_ref, tail_ref, buf, sh, *, tl):
    @pl.when(pl.program_id(1) == 0)
    def _():
        buf[0:CV_HIST, :] = hist_ref[0]

    buf[CV_HIST:CV_HIST + tl, :] = a_ref[...] * jax.nn.sigmoid(gt_ref[...])
    n_sh = tl + CV_HIST - SUBLANES
    for r in range(1, SUBLANES):
        sh[r - 1, 0:n_sh, :] = buf[r:r + n_sh, :]
    rb = min(CV_RB, tl)
    pad = CV_HIST - (CV_WIDTH - 1)
    for r0 in range(0, tl, rb):
        acc = jnp.broadcast_to(b_ref[...], (rb, D_GROUP))
        for k in range(CV_WIDTH):
            m, r = divmod(k + pad, SUBLANES)
            lo = r0 + m * SUBLANES
            tap = buf[lo:lo + rb, :] if r == 0 else sh[r - 1, lo:lo + rb, :]
            acc = acc + w_ref[k:k + 1, :] * tap
        y_ref[r0:r0 + rb, :] = _silu(_ln(acc, lg_ref[...], lb_ref[...])).astype(y_ref.dtype)
    tail = buf[tl:tl + CV_HIST, :]
    buf[0:CV_HIST, :] = tail
    tail_ref[0] = tail


def _cconv(h, hist, p, *, nb, seqlen, tl, row0, out_rows, orow0=0):
    rb0 = row0 // tl
    ob0 = orow0 // tl
    nt = seqlen // tl
    hspec = lambda cb: pl.BlockSpec((tl, D_GROUP), lambda b, t, cb=cb: (rb0 + b * nt + t, cb))
    full = lambda shape: pl.BlockSpec(shape, lambda b, t: (0,) * len(shape))
    kern = functools.partial(_cconv_kernel, tl=tl)
    return pl.pallas_call(
        kern,
        grid=(nb, nt),
        in_specs=[hspec(CB_CA), hspec(CB_CG), pl.BlockSpec((1, CV_HIST, D_GROUP), lambda b, t: (b, 0, 0)),
                  full((CV_HIST, D_GROUP)), full((1, D_GROUP)), full((1, D_GROUP)), full((1, D_GROUP))],
        out_specs=[pl.BlockSpec((tl, D_GROUP), lambda b, t: (ob0 + b * nt + t, 0)),
                   pl.BlockSpec((1, CV_HIST, D_GROUP), lambda b, t: (b, 0, 0))],
        out_shape=[jax.ShapeDtypeStruct((out_rows, D_GROUP), BF16),
                   jax.ShapeDtypeStruct((nb, CV_HIST, D_GROUP), F32)],
        scratch_shapes=[pltpu.VMEM((tl + CV_HIST, D_GROUP), F32),
                        pltpu.VMEM((SUBLANES - 1, tl + CV_HIST - SUBLANES, D_GROUP), F32)],
        compiler_params=_cp(("parallel", "arbitrary")),
        name=p["name"],
    )(h, h, hist, p["w"], p["b"], p["lg"], p["lb"])


R_ROWS = 32


def _outproj_kernel(*refs, n_full, rem):
    yp_refs, ys_refs = refs[0:4], refs[4:8]
    (x_ref, w_ref, g_ref, b_ref, wr_ref, br_ref, x1_ref, gcol_ref, eid_ref, cnt_ref, run_ref, ym_ref) = refs[8:]
    i = pl.program_id(0)
    tm = x_ref.shape[0]

    @pl.when(i == 0)
    def _():
        run_ref[...] = jnp.zeros(run_ref.shape, F32)

    @pl.when(i < n_full)
    def _():
        for m in range(4):
            ym_ref[m] = yp_refs[m][...]

    @pl.when(i >= n_full)
    def _():
        for m in range(4):
            if rem:
                ym_ref[m, 0:rem, :] = yp_refs[m][0:rem, :]
            ym_ref[m, rem:tm, :] = ys_refs[m][...]

    acc = jnp.dot(ym_ref[0], w_ref[0], preferred_element_type=F32)
    for m in range(1, 4):
        acc = acc + jnp.dot(ym_ref[m], w_ref[m], preferred_element_type=F32)
    x1 = _ln(ALPHA * x_ref[...] + acc, g_ref[...], b_ref[...])
    x1_ref[...] = x1
    lt = lax.dot_general(wr_ref[...], x1, NT, precision=HI, preferred_element_type=F32) + br_ref[...]
    lg = lt[0:N_EGROUPS, :]
    ridx = lax.broadcasted_iota(jnp.int32, (N_EGROUPS, tm), 0).astype(F32)

    def softmax0(a):
        e = jnp.exp(a - jnp.max(a, 0, keepdims=True))
        return e / jnp.sum(e, 0, keepdims=True)

    def top1(pv):
        best = jnp.max(pv, 0, keepdims=True)
        return best, jnp.min(jnp.where(pv == best, ridx, float(N_EGROUPS)), 0, keepdims=True)

    p_g, g_idx = top1(softmax0(lg))
    le = jnp.zeros((EXP_PER_GROUP, tm), F32)
    for g in range(N_EGROUPS):
        r0 = SUBLANES + g * EXP_PER_GROUP
        le = le + jnp.where(g_idx == g, lt[r0:r0 + EXP_PER_GROUP, :], 0.0)
    pe = softmax0(le)
    p1, i1 = top1(pe)
    p2, i2 = top1(jnp.where(ridx == i1, -1.0, pe))
    den = p1 + p2
    e1 = g_idx * EXP_PER_GROUP + i1
    e2 = g_idx * EXP_PER_GROUP + i2
    srow = lax.broadcasted_iota(jnp.int32, (SUBLANES, tm), 0)
    gates8 = jnp.where(srow == 0, p_g * p1 / den, jnp.where(srow == 1, p_g * p2 / den, 0.0))
    eye8 = (lax.broadcasted_iota(jnp.int32, (SUBLANES, LANES), 0)
            == lax.broadcasted_iota(jnp.int32, (SUBLANES, LANES), 1)).astype(F32)
    gcol_ref[...] = lax.dot_general(gates8, eye8, TN, precision=HI, preferred_element_type=F32)
    eio = lax.broadcasted_iota(jnp.int32, (N_EXPERTS, tm), 0).astype(F32)
    sel0 = eio == e1
    sel1 = eio == e2
    before = (lax.broadcasted_iota(jnp.int32, (tm, tm), 0)
              < lax.broadcasted_iota(jnp.int32, (tm, tm), 1))
    before = jnp.where(before, 1.0, 0.0).astype(BF16)
    oh0 = jnp.where(sel0, 1.0, 0.0)
    oh1 = jnp.where(sel1, 1.0, 0.0)
    cum0 = jnp.dot(oh0.astype(BF16), before, preferred_element_type=F32)
    cum1 = jnp.dot(oh1.astype(BF16), before, preferred_element_type=F32)
    tot0 = jnp.sum(oh0, 1, keepdims=True)
    tot1 = jnp.sum(oh1, 1, keepdims=True)
    base = run_ref[...]
    r0 = jnp.sum(jnp.where(sel0, base + cum0, 0.0), 0, keepdims=True)
    r1 = jnp.sum(jnp.where(sel1, base + tot0 + cum1, 0.0), 0, keepdims=True)
    run_ref[...] = base + tot0 + tot1
    cnt_ref[...] = jnp.broadcast_to(base + tot0 + tot1, cnt_ref.shape)
    eid_ref[...] = jnp.zeros(eid_ref.shape, jnp.int32)
    eid_ref[0:1, :] = e1.astype(jnp.int32)
    eid_ref[1:2, :] = e2.astype(jnp.int32)
    eid_ref[2:3, :] = r0.astype(jnp.int32)
    eid_ref[3:4, :] = r1.astype(jnp.int32)


def _outproj(ys_prompt, ys_sample, x, w_out, g, b, wr, br):
    t = x.shape[0]
    tp, ts = ys_prompt[0].shape[0], ys_sample[0].shape[0]
    tm = _pick(t, (640, 128))
    n_full, rem = divmod(tp, tm)
    assert rem + ts == tm and t == tp + ts, "sample rows must complete the last token tile"
    last_pb = pl.cdiv(tp, tm) - 1
    yspec = pl.BlockSpec((tm, D_GROUP), lambda i: (jnp.minimum(i, last_pb), 0))
    sspec = pl.BlockSpec((ts, D_GROUP), lambda i: (0, 0))
    full = lambda shape: pl.BlockSpec(shape, lambda i: (0,) * len(shape))
    kern = functools.partial(_outproj_kernel, n_full=n_full, rem=rem)
    return pl.pallas_call(
        kern,
        grid=(t // tm,),
        in_specs=[yspec, yspec, yspec, yspec, sspec, sspec, sspec, sspec,
                  pl.BlockSpec((tm, D_MODEL), lambda i: (i, 0)),
                  full((4, D_GROUP, D_MODEL)), full((1, D_MODEL)), full((1, D_MODEL)),
                  full((R_ROWS, D_MODEL)), full((R_ROWS, 1))],
        out_specs=[pl.BlockSpec((tm, D_MODEL), lambda i: (i, 0)),
                   pl.BlockSpec((tm, LANES), lambda i: (i, 0)),
                   pl.BlockSpec((SUBLANES, tm), lambda i: (0, i)),
                   full((N_EXPERTS, LANES))],
        out_shape=[jax.ShapeDtypeStruct((t, D_MODEL), F32), jax.ShapeDtypeStruct((t, LANES), F32),
                   jax.ShapeDtypeStruct((SUBLANES, t), jnp.int32),
                   jax.ShapeDtypeStruct((N_EXPERTS, LANES), F32)],
        scratch_shapes=[pltpu.VMEM((N_EXPERTS, 1), F32), pltpu.VMEM((4, tm, D_GROUP), BF16)],
        compiler_params=_cp(("arbitrary",)),
        name="outproj_ln_router",
    )(*ys_prompt, *ys_sample, x, w_out, g, b, wr, br)


def _moe_kernel(te_ref, nu_ref, x_ref, w1_ref, w3_ref, w2_ref, y_ref, w1b, w3b, w2b):
    i = pl.program_id(0)
    n_used = nu_ref[0]

    @pl.when(i < n_used)
    def _():
        @pl.when((i == 0) | (te_ref[i] != te_ref[jnp.maximum(i - 1, 0)]))
        def _():
            w1b[...] = w1_ref[0].astype(BF16)
            w3b[...] = w3_ref[0].astype(BF16)
            w2b[...] = w2_ref[0].astype(BF16)

        x = x_ref[...].astype(BF16)
        h1 = jnp.dot(x, w1b[...], preferred_element_type=F32)
        h3 = jnp.dot(x, w3b[...], preferred_element_type=F32)
        hh = (_silu(h1) * h3).astype(BF16)
        y_ref[...] = jnp.dot(hh, w2b[...], preferred_element_type=F32)

    @pl.when(i >= n_used)
    def _():
        y_ref[...] = jnp.zeros(y_ref.shape, F32)


def _moe(tile_expert, n_used, x_sorted, w1, w3, w2, *, n_tiles, layer):
    tm = MOE_TM
    xmap = lambda i, te, nu: (jnp.maximum(jnp.minimum(i, nu[0] - 1), 0), 0)
    wmap = lambda i, te, nu: (layer * N_EXPERTS + te[i], 0, 0)
    grid_spec = pltpu.PrefetchScalarGridSpec(
        num_scalar_prefetch=2,
        grid=(n_tiles,),
        in_specs=[pl.BlockSpec((tm, D_MODEL), xmap),
                  pl.BlockSpec((1, D_MODEL, D_EXPERT), wmap),
                  pl.BlockSpec((1, D_MODEL, D_EXPERT), wmap),
                  pl.BlockSpec((1, D_EXPERT, D_MODEL), wmap)],
        out_specs=pl.BlockSpec((tm, D_MODEL), lambda i, te, nu: (i, 0)),
        scratch_shapes=[pltpu.VMEM((D_MODEL, D_EXPERT), BF16), pltpu.VMEM((D_MODEL, D_EXPERT), BF16),
                        pltpu.VMEM((D_EXPERT, D_MODEL), BF16)],
    )
    return pl.pallas_call(
        _moe_kernel,
        grid_spec=grid_spec,
        out_shape=jax.ShapeDtypeStruct((n_tiles * tm, D_MODEL), F32),
        compiler_params=_cp(("arbitrary",)),
        name="moe_grouped",
    )(tile_expert, n_used, x_sorted, w1, w3, w2)


def _disp_row_copy(x_ref, xs_hbm, sem, r, row):
    return pltpu.make_async_copy(x_ref.at[pl.ds(r, 1), :], xs_hbm.at[pl.ds(row, 1), :], sem)


def _disp_tile_copy(zbuf, xs_hbm, sem, tile, tm):
    return pltpu.make_async_copy(zbuf, xs_hbm.at[pl.ds(pl.multiple_of(tile * tm, tm), tm), :], sem)


def _dispatch_kernel(pos_ref, plo_ref, phi_ref, nu_ref, x_ref, xs_hbm, zbuf, sem, zsem, tsem, *, tm, t, mt, n_tiles):
    i = pl.program_id(0)

    def body(r, c):
        for k in range(2):
            _disp_row_copy(x_ref, xs_hbm, sem, r, pos_ref[k * t + i * tm + r]).start()
        return c
    lax.fori_loop(0, tm, body, 0, unroll=DMA_UNROLL)

    @pl.when(i == 0)
    def _():
        zbuf[...] = jnp.zeros(zbuf.shape, F32)
        for e in range(N_EXPERTS):
            def zbody(row, c):
                _disp_row_copy(zbuf, xs_hbm, zsem, 0, row).start()
                return c
            lax.fori_loop(plo_ref[e], phi_ref[e], zbody, 0)

        def tbody(tile, c):
            _disp_tile_copy(zbuf, xs_hbm, tsem, tile, mt).start()
            return c
        lax.fori_loop(nu_ref[0], n_tiles, tbody, 0)
        for e in range(N_EXPERTS):
            def zwait(row, c):
                _disp_row_copy(zbuf, xs_hbm, zsem, 0, row).wait()
                return c
            lax.fori_loop(plo_ref[e], phi_ref[e], zwait, 0)

        def twait(tile, c):
            _disp_tile_copy(zbuf, xs_hbm, tsem, tile, mt).wait()
            return c
        lax.fori_loop(nu_ref[0], n_tiles, twait, 0)

    def wbody(r, c):
        for k in range(2):
            _disp_row_copy(x_ref, xs_hbm, sem, r, 0).wait()
        return c
    lax.fori_loop(0, tm, wbody, 0, unroll=DMA_UNROLL)


def _dispatch(pos, pad_lo, pad_hi, n_used, x1, *, n_tiles):
    t = x1.shape[0]
    tm = _pick(t, (640, 128))
    mt = MOE_TM
    kern = functools.partial(_dispatch_kernel, tm=tm, t=t, mt=mt, n_tiles=n_tiles)
    grid_spec = pltpu.PrefetchScalarGridSpec(
        num_scalar_prefetch=4,
        grid=(t // tm,),
        in_specs=[pl.BlockSpec((tm, D_MODEL), lambda i, *_: (i, 0))],
        out_specs=pl.BlockSpec(memory_space=pl.ANY),
        scratch_shapes=[pltpu.VMEM((mt, D_MODEL), F32), pltpu.SemaphoreType.DMA(()),
                        pltpu.SemaphoreType.DMA(()), pltpu.SemaphoreType.DMA(())],
    )
    return pl.pallas_call(
        kern,
        grid_spec=grid_spec,
        out_shape=jax.ShapeDtypeStruct((n_tiles * mt, D_MODEL), F32),
        compiler_params=_cp(("arbitrary",)),
        name="moe_dispatch",
    )(pos, pad_lo, pad_hi, n_used, x1)


def _moe_plan(ids, cnt, n_tiles):
    tm = MOE_TM
    counts = cnt[:, 0].astype(jnp.int32)
    ntile_e = (counts + tm - 1) // tm
    tile_end = jnp.cumsum(ntile_e)
    row_start = (tile_end - ntile_e) * tm
    n_used = tile_end[-1]
    tiles = jnp.arange(n_tiles, dtype=jnp.int32)
    te = jnp.sum((tiles[:, None] >= tile_end[None, :]).astype(jnp.int32), 1)
    last_e = jnp.sum((n_used - 1 >= tile_end).astype(jnp.int32))
    te = jnp.where(tiles < n_used, te, last_e).astype(jnp.int32)
    eid, rank = ids[0:2], ids[2:4]
    sel = eid[:, :, None] == jnp.arange(N_EXPERTS, dtype=jnp.int32)
    pos = (jnp.sum(jnp.where(sel, row_start, 0), -1) + rank).reshape(-1).astype(jnp.int32)
    pad_lo = (row_start + counts).astype(jnp.int32)
    pad_hi = (row_start + ntile_e * tm).astype(jnp.int32)
    return te, n_used.reshape(1).astype(jnp.int32), pos, pad_lo, pad_hi


def _comb_row_copy(y_hbm, ybuf, sem, row, slot, k, r):
    return pltpu.make_async_copy(y_hbm.at[pl.ds(row, 1), :], ybuf.at[slot, k, pl.ds(r, 1), :], sem.at[slot])


def _combine_kernel(pos_ref, y_hbm, x1_ref, gc_ref, g_ref, b_ref, xf_ref, xb_ref, ybuf, sem, *, tm, t):
    i = pl.program_id(0)
    n = pl.num_programs(0)

    def start_tile(tile, slot):
        def body(r, c):
            for k in range(2):
                _comb_row_copy(y_hbm, ybuf, sem, pos_ref[k * t + tile * tm + r], slot, k, r).start()
            return c
        lax.fori_loop(0, tm, body, 0, unroll=DMA_UNROLL)

    @pl.when(i == 0)
    def _():
        start_tile(0, 0)

    @pl.when(i + 1 < n)
    def _():
        start_tile(i + 1, (i + 1) % 2)

    slot = i % 2

    def wbody(r, c):
        for k in range(2):
            _comb_row_copy(y_hbm, ybuf, sem, 0, slot, k, r).wait()
        return c
    lax.fori_loop(0, tm, wbody, 0, unroll=DMA_UNROLL)
    ffn = gc_ref[:, 0:1] * ybuf[slot, 0] + gc_ref[:, 1:2] * ybuf[slot, 1]
    x2 = _ln(ALPHA * x1_ref[...] + ffn, g_ref[...], b_ref[...])
    xf_ref[...] = x2
    xb_ref[...] = x2.astype(BF16)


def _combine(pos, y_sorted, x1, gcol, g, b):
    t = x1.shape[0]
    tm = _pick(t, (320, 128))
    kern = functools.partial(_combine_kernel, tm=tm, t=t)
    grid_spec = pltpu.PrefetchScalarGridSpec(
        num_scalar_prefetch=1,
        grid=(t // tm,),
        in_specs=[pl.BlockSpec(memory_space=pl.ANY),
                  pl.BlockSpec((tm, D_MODEL), lambda i, pos: (i, 0)),
                  pl.BlockSpec((tm, LANES), lambda i, pos: (i, 0)),
                  pl.BlockSpec((1, D_MODEL), lambda i, pos: (0, 0)),
                  pl.BlockSpec((1, D_MODEL), lambda i, pos: (0, 0))],
        out_specs=[pl.BlockSpec((tm, D_MODEL), lambda i, pos: (i, 0)),
                   pl.BlockSpec((tm, D_MODEL), lambda i, pos: (i, 0))],
        scratch_shapes=[pltpu.VMEM((2, 2, tm, D_MODEL), F32), pltpu.SemaphoreType.DMA((2,))],
    )
    return pl.pallas_call(
        kern,
        grid_spec=grid_spec,
        out_shape=[jax.ShapeDtypeStruct((t, D_MODEL), F32), jax.ShapeDtypeStruct((t, D_MODEL), BF16)],
        compiler_params=_cp(("arbitrary",)),
        name="moe_combine_ln",
    )(pos, y_sorted, x1, gcol, g, b)


def _row(v, n=None):
    v = v.reshape(1, -1).astype(F32)
    if n is not None and v.shape[1] < n:
        v = jnp.pad(v, ((0, 0), (0, n - v.shape[1])))
    return v


def kernel(x_prompt, x_sample, state_ssm_conv, state_ssm, cache_k, cache_v, state_conv, ln_in_g, ln_in_b, w_in, ssm_conv_w, ssm_conv_b, ssm_dt_bias, ssm_a_log, ssm_d, ssm_norm_g, da_lq1, da_lk1, da_lq2, da_lk2, da_norm_g, sg_ln_g, sg_ln_b, sg_w, sg_b, cv_w, cv_b, cv_ln_g, cv_ln_b, w_out, ln1_g, ln1_b, moe_wg_group, moe_bg_group, moe_wg_exp, moe_bg_exp, moe_w1, moe_w3, moe_w2, ln2_g, ln2_b):
    bp, tp, _ = x_prompt.shape
    nb, ls, _ = x_sample.shape
    depth = w_in.shape[0]
    past = cache_k.shape[2]
    assert bp == 1 and tp % 512 == 0 and ls % SUBLANES == 0 and ls <= CHUNK
    ts = nb * ls
    t = tp + ts
    n_tiles = (2 * t + N_EXPERTS * (MOE_TM - 1)) // MOE_TM + 1

    dt0 = D_GROUP + SSM_CONV_CH
    w1s = moe_w1.reshape(depth * N_EXPERTS, D_MODEL, D_EXPERT)
    w3s = moe_w3.reshape(depth * N_EXPERTS, D_MODEL, D_EXPERT)
    w2s = moe_w2.reshape(depth * N_EXPERTS, D_EXPERT, D_MODEL)
    wr = jnp.zeros((depth, R_ROWS, D_MODEL), F32)
    wr = wr.at[:, 0:N_EGROUPS].set(jnp.swapaxes(moe_wg_group, 1, 2))
    wr = wr.at[:, SUBLANES:SUBLANES + N_EXPERTS].set(jnp.swapaxes(moe_wg_exp, 1, 2))
    br = jnp.zeros((depth, R_ROWS, 1), F32)
    br = br.at[:, 0:N_EGROUPS, 0].set(moe_bg_group).at[:, SUBLANES:SUBLANES + N_EXPERTS, 0].set(moe_bg_exp)
    cache_k2 = cache_k.reshape(depth * nb, past, D_GROUP)
    cache_v2 = cache_v.reshape(depth * nb, past, D_GROUP)

    tables_p = _rope_tables(np.arange(tp))
    tables_s = _rope_tables(np.tile(past + np.arange(ls), nb))

    xf, xb = _ln_in(x_prompt.reshape(tp, D_MODEL), x_sample.reshape(ts, D_MODEL), ln_in_g, ln_in_b)

    zeros_hist = jnp.zeros((1, SUBLANES, SSM_CONV_CH), F32)
    zeros_state = jnp.zeros((1, SSM_STATE, D_GROUP), F32)
    zeros_cv = jnp.zeros((1, CV_HIST, D_GROUP), F32)
    outs = {k: [] for k in ("p_sc", "p_ss", "p_k", "p_v", "p_cv", "s_sc", "s_ss", "s_k", "s_v", "s_cv", "s_sg")}
    tl_p = 512
    tl_c = 256

    for l in range(depth):
        lam_init = 0.8 - 0.6 * math.exp(-0.3 * l)
        w_in_l = w_in[l]
        w_main = jnp.concatenate([w_in_l[:, :dt0], w_in_l[:, dt0 + SSM_HEADS:]], -1).astype(BF16)
        w_dt = jnp.pad(w_in_l[:, dt0:dt0 + SSM_HEADS], ((0, 0), (0, LANES - SSM_HEADS))).astype(BF16)
        w_out_b = w_out[l].reshape(4, D_GROUP, D_MODEL).astype(BF16)
        h, dt = _inproj(xb, w_main, w_dt)

        ssd_p = dict(cw=ssm_conv_w[l], cb=_row(ssm_conv_b[l]), dtb=_row(ssm_dt_bias[l], LANES),
                     a=_row(-jnp.exp(ssm_a_log[l].astype(F32)), LANES),
                     dsk=_row(jnp.repeat(ssm_d[l], SSM_HEAD_DIM)), ng=_row(ssm_norm_g[l]))
        ya, hist_p, st_p = _ssd(h, dt, zeros_hist, zeros_state, dict(ssd_p, len=tp, name="ssd_prompt"),
                                nb=1, tl=tl_p, q=CHUNK, row0=0, out_rows=tp)
        hist_s_in = jnp.pad(state_ssm_conv[l], ((0, 0), (SUBLANES - (SSM_CONV - 1), 0), (0, 0)))
        h0t_s = jnp.swapaxes(state_ssm[l].reshape(nb, D_GROUP, SSM_STATE), 1, 2)
        ya_s, hist_s, st_s = _ssd(h, dt, hist_s_in, h0t_s, dict(ssd_p, len=ls, name="ssd_sample"),
                                  nb=nb, tl=ls, q=ls, row0=tp, out_rows=ts)
        outs["p_sc"].append(hist_p[:, SUBLANES - (SSM_CONV - 1):])
        outs["s_sc"].append(hist_s[:, SUBLANES - (SSM_CONV - 1):])
        unt = lambda s: jnp.swapaxes(s, 1, 2).reshape(-1, SSM_HEADS, SSM_HEAD_DIM, SSM_STATE)
        outs["p_ss"].append(unt(st_p))
        outs["s_ss"].append(unt(st_s))

        qb, kf, kb, vb, vf = _rope(h, tables_p, row0=0, nrows=tp)
        qb_s, kf_s, kb_s, vb_s, vf_s = _rope(h, tables_s, row0=tp, nrows=ts)
        f32 = lambda a: a.astype(F32)
        lam = (jnp.exp(jnp.sum(f32(da_lq1[l]) * f32(da_lk1[l]))) - jnp.exp(jnp.sum(f32(da_lq2[l]) * f32(da_lk2[l])))
               + lam_init).reshape(1).astype(F32)
        gda = _row(da_norm_g[l])
        yb = _attn_prompt(lam, qb, kb, vb, gda, tp=tp, out_rows=tp, scale_out=1.0 - lam_init)
        yb_s = _attn_sample(lam, qb_s, kb_s, vb_s, cache_k2, cache_v2, gda, layer=l, nb=nb, ls=ls, row0=0,
                            scale_out=1.0 - lam_init)
        outs["p_k"].append(kf.reshape(1, tp, DA_HEADS, 2, DA_DIM))
        outs["s_k"].append(kf_s.reshape(nb, ls, DA_HEADS, 2, DA_DIM))
        outs["p_v"].append(vf.reshape(1, tp, DA_HEADS, DA_VDIM))
        outs["s_v"].append(vf_s.reshape(nb, ls, DA_HEADS, DA_VDIM))

        sg_common = dict(g=_row(sg_ln_g[l]), b=_row(sg_ln_b[l]))
        bias_full = lambda qq: jnp.repeat(sg_b[l][:, :qq].T, LANES, axis=1).astype(F32)
        (yc,) = _sgu(h, dict(sg_common, w=sg_w[l], bias=bias_full(SG_CHUNK), name="sgu_prompt"),
                     nrows=tp, tl=tl_p, q=SG_CHUNK, row0=0, out_rows=tp, want_v=False)
        yc_s, v_rows = _sgu(h, dict(sg_common, w=sg_w[l][:, :ls, :ls], bias=bias_full(ls), name="sgu_sample"),
                            nrows=ts, tl=ls, q=ls, row0=tp, out_rows=ts, want_v=True)
        outs["s_sg"].append(v_rows.reshape(nb, ls, D_GROUP))

        cv_p = dict(w=jnp.pad(cv_w[l], ((0, CV_HIST - CV_WIDTH), (0, 0))), b=_row(cv_b[l]),
                    lg=_row(cv_ln_g[l]), lb=_row(cv_ln_b[l]))
        yd, tail_p = _cconv(h, zeros_cv, dict(cv_p, name="cconv_prompt"), nb=1, seqlen=tp, tl=tl_c, row0=0,
                            out_rows=tp)
        cv_hist_s = jnp.pad(state_conv[l], ((0, 0), (CV_HIST - (CV_WIDTH - 1), 0), (0, 0)))
        yd_s, tail_s = _cconv(h, cv_hist_s, dict(cv_p, name="cconv_sample"), nb=nb, seqlen=ls, tl=ls, row0=tp,
                              out_rows=ts)
        outs["p_cv"].append(tail_p[:, CV_HIST - (CV_WIDTH - 1):])
        outs["s_cv"].append(tail_s[:, CV_HIST - (CV_WIDTH - 1):])

        x1, gcol, ids, cnt = _outproj((ya, yb, yc, yd), (ya_s, yb_s, yc_s, yd_s), xf, w_out_b, _row(ln1_g[l]), _row(ln1_b[l]), wr[l], br[l])
        te, n_used, pos, pad_lo, pad_hi = _moe_plan(ids, cnt, n_tiles)
        x_sorted = _dispatch(pos, pad_lo, pad_hi, n_used, x1, n_tiles=n_tiles)
        y_sorted = _moe(te, n_used, x_sorted, w1s, w3s, w2s, n_tiles=n_tiles, layer=l)
        xf, xb = _combine(pos, y_sorted, x1, gcol, _row(ln2_g[l]), _row(ln2_b[l]))

    st = lambda k: jnp.stack(outs[k])
    return (xf[:tp].reshape(1, tp, D_MODEL), xf[tp:].reshape(nb, ls, D_MODEL),
            st("p_sc"), st("p_ss"), st("p_k"), st("p_v"), st("p_cv"),
            st("s_sc"), st("s_ss"), st("s_k"), st("s_v"), st("s_cv"), st("s_sg"))
```

```python
import functools
import math

import jax
import jax.numpy as jnp
import numpy as np
from jax import lax
from jax.experimental import pallas as pl
from jax.experimental.pallas import tpu as pltpu

F32 = jnp.float32
BF16 = jnp.bfloat16
HI = lax.Precision.HIGHEST

D_MODEL = 2048
DEPTH = 4
CHUNK = 64
D_GROUP = 512
SSM_HEADS = 8
SSM_HEAD_DIM = 64
SSM_GROUPS = 2
SSM_STATE = 128
SSM_CONV = 4
SSM_CONV_CH = 1024
DA_HEADS = 4
DA_DIM = 64
DA_VDIM = 128
ROPE_DIM = 16
ROPE_THETA = 500000.0
SG_CHUNK = 128
SG_GROUPS = 4
CV_WIDTH = 31
N_EGROUPS = 4
EXP_PER_GROUP = 4
N_EXPERTS = 16
D_EXPERT = 512
ALPHA = (2 * DEPTH) ** 0.25
EPS = 1e-5

LANES = 128
SUBLANES = 8
VMEM_LIMIT = 56 * 1024 * 1024
D_MAIN = 10 * D_GROUP
MOE_TM = 256
DMA_UNROLL = 8
NT = (((1,), (1,)), ((), ()))
TN = (((0,), (0,)), ((), ()))

CB_Z, CB_XS, CB_BC, CB_Q, CB_K, CB_V, CB_SU, CB_SV, CB_CA, CB_CG = range(10)


def _cp(sem):
    return pltpu.CompilerParams(dimension_semantics=sem, vmem_limit_bytes=VMEM_LIMIT)


def _pick(n, cands):
    for c in cands:
        if n % c == 0:
            return c
    raise ValueError(f"no tile for {n}")


def _ln(x, g, b):
    mu = jnp.mean(x, -1, keepdims=True)
    xc = x - mu
    var = jnp.mean(xc * xc, -1, keepdims=True)
    return xc * lax.rsqrt(var + EPS) * g + b


def _silu(x):
    return x * jax.nn.sigmoid(x)


def _softplus(x):
    return jnp.maximum(x, 0.0) + jnp.log1p(jnp.exp(-jnp.abs(x)))


def _ln_in_kernel(xp_ref, xs_ref, g_ref, b_ref, xf_ref, xb_ref, *, n_full, rem):
    i = pl.program_id(0)
    tm = xf_ref.shape[0]

    def emit(lo, hi, x):
        y = _ln(x, g_ref[...], b_ref[...])
        xf_ref[lo:hi, :] = y
        xb_ref[lo:hi, :] = y.astype(BF16)

    @pl.when(i < n_full)
    def _():
        emit(0, tm, xp_ref[...])

    @pl.when(i >= n_full)
    def _():
        if rem:
            emit(0, rem, xp_ref[0:rem, :])
        emit(rem, tm, xs_ref[...])


def _ln_in(xp, xs, g, b):
    tp, ts = xp.shape[0], xs.shape[0]
    t = tp + ts
    tm = _pick(t, (640, 128))
    n_full, rem = divmod(tp, tm)
    assert rem + ts == tm, "sample rows must complete the last token tile"
    last_pb = pl.cdiv(tp, tm) - 1
    kern = functools.partial(_ln_in_kernel, n_full=n_full, rem=rem)
    return pl.pallas_call(
        kern,
        grid=(t // tm,),
        in_specs=[pl.BlockSpec((tm, D_MODEL), lambda i: (jnp.minimum(i, last_pb), 0)),
                  pl.BlockSpec((ts, D_MODEL), lambda i: (0, 0)),
                  pl.BlockSpec((1, D_MODEL), lambda i: (0, 0)),
                  pl.BlockSpec((1, D_MODEL), lambda i: (0, 0))],
        out_specs=[pl.BlockSpec((tm, D_MODEL), lambda i: (i, 0)),
                   pl.BlockSpec((tm, D_MODEL), lambda i: (i, 0))],
        out_shape=[jax.ShapeDtypeStruct((t, D_MODEL), F32), jax.ShapeDtypeStruct((t, D_MODEL), BF16)],
        compiler_params=_cp(("parallel",)),
        name="ln_in",
    )(xp, xs, g.reshape(1, -1), b.reshape(1, -1))


def _inproj_kernel(x_ref, w_ref, wdt_ref, h_ref, dt_ref):
    x = x_ref[...]
    h_ref[...] = jnp.dot(x, w_ref[...], preferred_element_type=F32)

    @pl.when(pl.program_id(1) == 0)
    def _():
        dt_ref[...] = jnp.dot(x, wdt_ref[...], preferred_element_type=F32)


def _inproj(xb, w_main, w_dt):
    t = xb.shape[0]
    tm = _pick(t, (1664, 1024, 128))
    tn = D_GROUP
    return pl.pallas_call(
        _inproj_kernel,
        grid=(t // tm, D_MAIN // tn),
        in_specs=[pl.BlockSpec((tm, D_MODEL), lambda i, j: (i, 0)),
                  pl.BlockSpec((D_MODEL, tn), lambda i, j: (0, j)),
                  pl.BlockSpec((D_MODEL, LANES), lambda i, j: (0, 0))],
        out_specs=[pl.BlockSpec((tm, tn), lambda i, j: (i, j)),
                   pl.BlockSpec((tm, LANES), lambda i, j: (i, 0))],
        out_shape=[jax.ShapeDtypeStruct((t, D_MAIN), F32), jax.ShapeDtypeStruct((t, LANES), F32)],
        compiler_params=_cp(("parallel", "arbitrary")),
        name="inproj",
    )(xb, w_main, w_dt)


def _rope_kernel(q_ref, k_ref, v_ref, c_ref, s1_ref, s2_ref, qb_ref, kf_ref, kb_ref, vb_ref, vf_ref):
    c = c_ref[...]
    s1 = s1_ref[...]
    s2 = s2_ref[...]

    def rot(x):
        return x * c + pltpu.roll(x, LANES - ROPE_DIM // 2, 1) * s1 + pltpu.roll(x, ROPE_DIM // 2, 1) * s2

    for h in range(DA_HEADS):
        sl = slice(h * LANES, (h + 1) * LANES)
        qr = rot(q_ref[:, sl])
        kr = rot(k_ref[:, sl])
        qb_ref[:, sl] = (qr * (DA_DIM ** -0.5)).astype(BF16)
        kf_ref[:, sl] = kr
        kb_ref[:, sl] = kr.astype(BF16)
    v = v_ref[...]
    vb_ref[...] = v.astype(BF16)
    vf_ref[...] = v


def _rope(h, tables, *, row0, nrows):
    tm = _pick(nrows, (512, 128))
    rb0 = row0 // tm
    hspec = lambda cb: pl.BlockSpec((tm, D_GROUP), lambda i, cb=cb: (rb0 + i, cb))
    tspec = pl.BlockSpec((tm, LANES), lambda i: (i, 0))
    ospec = pl.BlockSpec((tm, D_GROUP), lambda i: (i, 0))
    return pl.pallas_call(
        _rope_kernel,
        grid=(nrows // tm,),
        in_specs=[hspec(CB_Q), hspec(CB_K), hspec(CB_V), tspec, tspec, tspec],
        out_specs=[ospec, ospec, ospec, ospec, ospec],
        out_shape=[jax.ShapeDtypeStruct((nrows, D_GROUP), BF16), jax.ShapeDtypeStruct((nrows, D_GROUP), F32),
                   jax.ShapeDtypeStruct((nrows, D_GROUP), BF16), jax.ShapeDtypeStruct((nrows, D_GROUP), BF16),
                   jax.ShapeDtypeStruct((nrows, D_GROUP), F32)],
        compiler_params=_cp(("parallel",)),
        name="rope",
    )(h, h, h, *tables)


def _rope_tables(pos):
    half = ROPE_DIM // 2
    inv_freq = 1.0 / (ROPE_THETA ** (np.arange(half, dtype=np.float64) * 2.0 / ROPE_DIM))
    ang = np.asarray(pos, np.float64)[:, None] * inv_freq
    cos, sin = np.cos(ang), np.sin(ang)
    n = ang.shape[0]
    one = np.ones((n, DA_DIM - ROPE_DIM))
    zero = np.zeros((n, DA_DIM - ROPE_DIM))
    z8 = np.zeros((n, half))
    c = np.concatenate([cos, cos, one], 1)
    s1 = np.concatenate([-sin, z8, zero], 1)
    s2 = np.concatenate([z8, sin, zero], 1)
    return tuple(jnp.asarray(np.concatenate([a, a], 1), F32) for a in (c, s1, s2))


def _ssd_kernel(z_ref, xs_ref, bc_ref, dt_ref, hist_ref, h0_ref, cw_ref, cb_ref, dtb_ref, a_ref, dsk_ref, ng_ref,
                y_ref, hist_o_ref, st_o_ref, xbuf, act, dts, ht, *, tl, q):
    hp = SSM_HEADS // SSM_GROUPS
    gw = hp * SSM_HEAD_DIM

    @pl.when(pl.program_id(1) == 0)
    def _():
        xbuf[0:SUBLANES, :] = hist_ref[0]
        ht[...] = h0_ref[0]

    xbuf[SUBLANES:SUBLANES + tl, 0:D_GROUP] = xs_ref[...]
    xbuf[SUBLANES:SUBLANES + tl, D_GROUP:2 * D_GROUP] = bc_ref[...]
    acc = cb_ref[...] + cw_ref[SSM_CONV - 1:SSM_CONV, :] * xbuf[SUBLANES:SUBLANES + tl, :]
    for k in range(SSM_CONV - 1):
        off = SUBLANES - (SSM_CONV - 1) + k
        acc = acc + cw_ref[k:k + 1, :] * xbuf[off:off + tl, :]
    act[...] = _silu(acc)
    tail = xbuf[tl:tl + SUBLANES, :]
    xbuf[0:SUBLANES, :] = tail
    hist_o_ref[0] = tail
    dts[...] = _softplus(dt_ref[...] + dtb_ref[...])

    ri = lax.broadcasted_iota(jnp.int32, (q, q), 0)
    ci = lax.broadcasted_iota(jnp.int32, (q, q), 1)
    causal = ri >= ci
    tril = causal.astype(F32)
    expand = (lax.broadcasted_iota(jnp.int32, (LANES, D_GROUP), 1) // SSM_HEAD_DIM
              == lax.broadcasted_iota(jnp.int32, (LANES, D_GROUP), 0)).astype(F32)
    eye8 = (lax.broadcasted_iota(jnp.int32, (SUBLANES, LANES), 0)
            == lax.broadcasted_iota(jnp.int32, (SUBLANES, LANES), 1)).astype(F32)
    lane_head = lax.broadcasted_iota(jnp.int32, (q, gw), 1) // SSM_HEAD_DIM

    def chunk(ci_, carry):
        r0 = pl.multiple_of(ci_ * q, q)
        dt_c = dts[pl.ds(r0, q), :]
        da = dt_c * a_ref[...]
        cs = jnp.dot(tril, da, precision=HI, preferred_element_type=F32)
        csx = jnp.dot(cs, expand, precision=HI, preferred_element_type=F32)
        dtx = jnp.dot(dt_c, expand, precision=HI, preferred_element_type=F32)
        cst = lax.dot_general(eye8, cs, NT, precision=HI, preferred_element_type=F32)
        dtt = lax.dot_general(eye8, dt_c, NT, precision=HI, preferred_element_type=F32)
        xs = act[pl.ds(r0, q), 0:D_GROUP]
        ecs = jnp.exp(csx)
        cs_last = csx[q - 1:q, :]
        xw = (xs * (jnp.exp(cs_last - csx) * dtx)).astype(BF16)
        xsb = xs.astype(BF16)
        dec = jnp.exp(cs_last)
        ys = []
        for g in range(SSM_GROUPS):
            gl = slice(g * gw, (g + 1) * gw)
            bg = act[pl.ds(r0, q), D_GROUP + g * SSM_STATE:D_GROUP + (g + 1) * SSM_STATE].astype(BF16)
            c0 = D_GROUP + SSM_GROUPS * SSM_STATE
            cg = act[pl.ds(r0, q), c0 + g * SSM_STATE:c0 + (g + 1) * SSM_STATE].astype(BF16)
            cb = lax.dot_general(cg, bg, NT, preferred_element_type=F32)
            hin = ht[:, gl]
            yg = jnp.dot(cg, hin.astype(BF16), preferred_element_type=F32) * ecs[:, gl]
            xg = xsb[:, gl]
            for hh in range(hp):
                h = g * hp + hh
                seg = cs[:, h:h + 1] - cst[h:h + 1, :]
                lm = jnp.exp(jnp.where(causal, seg, -jnp.inf))
                m = (cb * lm * dtt[h:h + 1, :]).astype(BF16)
                xm = jnp.where(lane_head == hh, xg, jnp.zeros_like(xg))
                yg = yg + jnp.dot(m, xm, preferred_element_type=F32)
            st = lax.dot_general(bg, xw[:, gl], TN, preferred_element_type=F32)
            ht[:, gl] = dec[:, gl] * hin + st
            ys.append(yg)
        y = jnp.concatenate(ys, axis=1) + dsk_ref[...] * xs
        y = y * _silu(z_ref[pl.ds(r0, q), :])
        outs = []
        for g in range(SSM_GROUPS):
            yg = y[:, g * gw:(g + 1) * gw]
            outs.append(yg * lax.rsqrt(jnp.mean(yg * yg, -1, keepdims=True) + EPS))
        y = jnp.concatenate(outs, axis=1) * ng_ref[...]
        y_ref[pl.ds(r0, q), :] = y.astype(y_ref.dtype)
        return carry

    lax.fori_loop(0, tl // q, chunk, 0)
    st_o_ref[0] = ht[...]


def _ssd(h, dt, hist, h0t, p, *, nb, tl, q, row0, out_rows, orow0=0):
    rb0 = row0 // tl
    ob0 = orow0 // tl
    nt = (p["len"]) // tl

    def hspec(cb):
        return pl.BlockSpec((tl, D_GROUP), lambda b, t, cb=cb: (rb0 + b * nt + t, cb))

    full = lambda shape: pl.BlockSpec(shape, lambda b, t: (0,) * len(shape))
    kern = functools.partial(_ssd_kernel, tl=tl, q=q)
    return pl.pallas_call(
        kern,
        grid=(nb, nt),
        in_specs=[hspec(CB_Z), hspec(CB_XS), hspec(CB_BC),
                  pl.BlockSpec((tl, LANES), lambda b, t: (rb0 + b * nt + t, 0)),
                  pl.BlockSpec((1, SUBLANES, SSM_CONV_CH), lambda b, t: (b, 0, 0)),
                  pl.BlockSpec((1, SSM_STATE, D_GROUP), lambda b, t: (b, 0, 0)),
                  full((SSM_CONV, SSM_CONV_CH)), full((1, SSM_CONV_CH)), full((1, LANES)), full((1, LANES)),
                  full((1, D_GROUP)), full((1, D_GROUP))],
        out_specs=[pl.BlockSpec((tl, D_GROUP), lambda b, t: (ob0 + b * nt + t, 0)),
                   pl.BlockSpec((1, SUBLANES, SSM_CONV_CH), lambda b, t: (b, 0, 0)),
                   pl.BlockSpec((1, SSM_STATE, D_GROUP), lambda b, t: (b, 0, 0))],
        out_shape=[jax.ShapeDtypeStruct((out_rows, D_GROUP), BF16),
                   jax.ShapeDtypeStruct((nb, SUBLANES, SSM_CONV_CH), F32),
                   jax.ShapeDtypeStruct((nb, SSM_STATE, D_GROUP), F32)],
        scratch_shapes=[pltpu.VMEM((tl + SUBLANES, SSM_CONV_CH), F32), pltpu.VMEM((tl, SSM_CONV_CH), F32),
                        pltpu.VMEM((tl, LANES), F32), pltpu.VMEM((SSM_STATE, D_GROUP), F32)],
        compiler_params=_cp(("parallel", "arbitrary")),
        name=p["name"],
    )(h, h, h, dt, hist, h0t, p["cw"], p["cb"], p["dtb"], p["a"], p["dsk"], p["ng"])


def _lane_fold(x, op):
    parts = [x[:, t * LANES:(t + 1) * LANES] for t in range(x.shape[1] // LANES)]
    while len(parts) > 1:
        parts = [op(parts[a], parts[a + 1]) for a in range(0, len(parts), 2)]
    return parts[0]


def _attn_kernel(it_ref, jt_ref, pt_ref, lam_ref, q_ref, k_ref, v_ref, g_ref, o_ref,
                 mx_ref, mrep_ref, ls_ref, acc_ref, *, tq, scale_out):
    step = pl.program_id(1)
    i = it_ref[step]
    j = jt_ref[step]
    sweep = pt_ref[step]
    nl = tq // LANES

    def scores(c, diagonal):
        sl = slice(c * DA_DIM, (c + 1) * DA_DIM)
        s = lax.dot_general(q_ref[:, sl], k_ref[:, sl], NT, preferred_element_type=F32)
        if diagonal:
            rowc = lax.broadcasted_iota(jnp.int32, (tq, tq), 0) // CHUNK
            colc = lax.broadcasted_iota(jnp.int32, (tq, tq), 1) // CHUNK
            s = jnp.where(colc <= rowc, s, -jnp.inf)
        return s

    def sweep0(diagonal):
        for c in range(2):
            mx_ref[c] = jnp.maximum(mx_ref[c], _lane_fold(scores(c, diagonal), jnp.maximum))
        if diagonal:
            for c in range(2):
                m = jnp.max(mx_ref[c], -1, keepdims=True)
                mrep_ref[c] = jnp.broadcast_to(m, (tq, LANES))
            ls_ref[...] = jnp.zeros(ls_ref.shape, F32)
            acc_ref[...] = jnp.zeros(acc_ref.shape, F32)

    def sweep1(diagonal):
        v = v_ref[...]
        for c in range(2):
            m = mrep_ref[c]
            p = jnp.exp(scores(c, diagonal) - jnp.concatenate([m] * nl, axis=1))
            ls_ref[c] = ls_ref[c] + _lane_fold(p, jnp.add)
            acc_ref[c] = acc_ref[c] + jnp.dot(p.astype(BF16), v, preferred_element_type=F32)
        if diagonal:
            l0 = jnp.sum(ls_ref[0], -1, keepdims=True)
            l1 = jnp.sum(ls_ref[1], -1, keepdims=True)
            o = acc_ref[0] / l0 - lam_ref[0] * (acc_ref[1] / l1)
            ms = jnp.mean(o * o, -1, keepdims=True)
            o_ref[...] = (o * lax.rsqrt(ms + EPS) * g_ref[...] * scale_out).astype(o_ref.dtype)

    @pl.when((sweep == 0) & (j == 0))
    def _():
        mx_ref[...] = jnp.full(mx_ref.shape, -jnp.inf, F32)

    for sw, fn in ((0, sweep0), (1, sweep1)):
        @pl.when((sweep == sw) & (j < i))
        def _(fn=fn):
            fn(False)

        @pl.when((sweep == sw) & (j == i))
        def _(fn=fn):
            fn(True)


def _attn_prompt(lam, qb, kb, vb, g, *, tp, out_rows, scale_out):
    tq = _pick(tp, (1024, 512, 256, 128))
    n = tp // tq
    steps = [(i, j, sw) for i in range(n) for sw in range(2) for j in range(i + 1)]
    it, jt, pt = (jnp.asarray([s[a] for s in steps], jnp.int32) for a in range(3))
    kern = functools.partial(_attn_kernel, tq=tq, scale_out=scale_out)
    grid_spec = pltpu.PrefetchScalarGridSpec(
        num_scalar_prefetch=3,
        grid=(DA_HEADS, len(steps)),
        in_specs=[pl.BlockSpec(memory_space=pltpu.SMEM),
                  pl.BlockSpec((tq, LANES), lambda h, s, it, jt, pt: (it[s], h)),
                  pl.BlockSpec((tq, LANES), lambda h, s, it, jt, pt: (jt[s], h)),
                  pl.BlockSpec((tq, LANES), lambda h, s, it, jt, pt: (jt[s] * pt[s], h)),
                  pl.BlockSpec((1, LANES), lambda h, s, it, jt, pt: (0, 0))],
        out_specs=pl.BlockSpec((tq, LANES), lambda h, s, it, jt, pt: (it[s], h)),
        scratch_shapes=[pltpu.VMEM((2, tq, LANES), F32), pltpu.VMEM((2, tq, LANES), F32),
                        pltpu.VMEM((2, tq, LANES), F32), pltpu.VMEM((2, tq, LANES), F32)],
    )
    return pl.pallas_call(
        kern,
        grid_spec=grid_spec,
        out_shape=jax.ShapeDtypeStruct((out_rows, D_GROUP), BF16),
        compiler_params=_cp(("parallel", "arbitrary")),
        name="attn_prompt",
    )(it, jt, pt, lam, qb, kb, vb, g)


def _attn_s_kernel(lam_ref, q_ref, kc_ref, vc_ref, kn_ref, vn_ref, g_ref, o_ref, *, scale_out):
    q = q_ref[...]
    kc = kc_ref[0].astype(BF16)
    vc = vc_ref[0].astype(BF16)
    kn = kn_ref[...]
    vn = vn_ref[...]
    outs = []
    for c in range(2):
        sl = slice(c * DA_DIM, (c + 1) * DA_DIM)
        sp = lax.dot_general(q[:, sl], kc[:, sl], NT, preferred_element_type=F32)
        sn = lax.dot_general(q[:, sl], kn[:, sl], NT, preferred_element_type=F32)
        m = jnp.maximum(jnp.max(sp, -1, keepdims=True), jnp.max(sn, -1, keepdims=True))
        pp = jnp.exp(sp - m)
        pn = jnp.exp(sn - m)
        l = jnp.sum(pp, -1, keepdims=True) + jnp.sum(pn, -1, keepdims=True)
        o = jnp.dot(pp.astype(BF16), vc, preferred_element_type=F32) + jnp.dot(pn.astype(BF16), vn,
                                                                               preferred_element_type=F32)
        outs.append(o / l)
    o = outs[0] - lam_ref[0] * outs[1]
    ms = jnp.mean(o * o, -1, keepdims=True)
    o_ref[...] = (o * lax.rsqrt(ms + EPS) * g_ref[...] * scale_out).astype(o_ref.dtype)


def _attn_sample(lam, qb, kb, vb, cache_k, cache_v, g, *, layer, nb, ls, row0, scale_out):
    past = cache_k.shape[1]
    rb0 = row0 // ls
    new = lambda: pl.BlockSpec((ls, LANES), lambda b, h: (rb0 + b, h))
    cache = lambda: pl.BlockSpec((1, past, LANES), lambda b, h: (layer * nb + b, 0, h))
    kern = functools.partial(_attn_s_kernel, scale_out=scale_out)
    return pl.pallas_call(
        kern,
        grid=(nb, DA_HEADS),
        in_specs=[pl.BlockSpec(memory_space=pltpu.SMEM),
                  new(), cache(), cache(), new(), new(), pl.BlockSpec((1, LANES), lambda b, h: (0, 0))],
        out_specs=pl.BlockSpec((ls, LANES), lambda b, h: (b, h)),
        out_shape=jax.ShapeDtypeStruct((nb * ls, D_GROUP), BF16),
        compiler_params=_cp(("parallel", "parallel")),
        name="attn_sample",
    )(lam, qb, cache_k, cache_v, kb, vb, g)


def _sgu_kernel(u_ref, v_ref, g_ref, b_ref, w_ref, bias_ref, y_ref, vo_ref, *, tl, q):
    v = _ln(v_ref[...], g_ref[...], b_ref[...])
    if vo_ref is not None:
        vo_ref[...] = v
    vb = v.astype(BF16)
    tri = lax.broadcasted_iota(jnp.int32, (q, q), 0) >= lax.broadcasted_iota(jnp.int32, (q, q), 1)
    for g in range(SG_GROUPS):
        w = jnp.where(tri, w_ref[g], 0.0).astype(BF16)
        sl = slice(g * LANES, (g + 1) * LANES)
        for c in range(tl // q):
            rs = slice(c * q, (c + 1) * q)
            s = jnp.dot(w, vb[rs, sl], preferred_element_type=F32) + bias_ref[:, sl]
            y_ref[rs, sl] = (u_ref[rs, sl] * s).astype(y_ref.dtype)


def _sgu(h, p, *, nrows, tl, q, row0, out_rows, want_v, orow0=0):
    rb0 = row0 // tl
    ob0 = orow0 // tl
    hspec = lambda cb: pl.BlockSpec((tl, D_GROUP), lambda i, cb=cb: (rb0 + i, cb))
    full = lambda shape: pl.BlockSpec(shape, lambda i: (0,) * len(shape))
    out_specs = [pl.BlockSpec((tl, D_GROUP), lambda i: (ob0 + i, 0))]
    out_shape = [jax.ShapeDtypeStruct((out_rows, D_GROUP), BF16)]
    if want_v:
        out_specs.append(pl.BlockSpec((tl, D_GROUP), lambda i: (i, 0)))
        out_shape.append(jax.ShapeDtypeStruct((nrows, D_GROUP), F32))
        kern = functools.partial(_sgu_kernel, tl=tl, q=q)
    else:
        kern = lambda u, v, g, b, w, bias, y: _sgu_kernel(u, v, g, b, w, bias, y, None, tl=tl, q=q)
    return pl.pallas_call(
        kern,
        grid=(nrows // tl,),
        in_specs=[hspec(CB_SU), hspec(CB_SV), full((1, D_GROUP)), full((1, D_GROUP)),
                  full((SG_GROUPS, q, q)), full((q, D_GROUP))],
        out_specs=out_specs,
        out_shape=out_shape,
        compiler_params=_cp(("parallel",)),
        name=p["name"],
    )(h, h, p["g"], p["b"], p["w"], p["bias"])


CV_HIST = 32
CV_RB = 64


def _cconv_kernel(a_ref, gt_ref, hist_ref, w_ref, b_ref, lg_ref, lb_ref, y_ref, tail_ref, buf, sh, *, tl):
    @pl.when(pl.program_id(1) == 0)
    def _():
        buf[0:CV_HIST, :] = hist_ref[0]

    buf[CV_HIST:CV_HIST + tl, :] = a_ref[...] * jax.nn.sigmoid(gt_ref[...])
    n_sh = tl + CV_HIST - SUBLANES
    for r in range(1, SUBLANES):
        sh[r - 1, 0:n_sh, :] = buf[r:r + n_sh, :]
    rb = min(CV_RB, tl)
    pad = CV_HIST - (CV_WIDTH - 1)
    for r0 in range(0, tl, rb):
        acc = jnp.broadcast_to(b_ref[...], (rb, D_GROUP))
        for k in range(CV_WIDTH):
            m, r = divmod(k + pad, SUBLANES)
            lo = r0 + m * SUBLANES
            tap = buf[lo:lo + rb, :] if r == 0 else sh[r - 1, lo:lo + rb, :]
            acc = acc + w_ref[k:k + 1, :] * tap
        y_ref[r0:r0 + rb, :] = _silu(_ln(acc, lg_ref[...], lb_ref[...])).astype(y_ref.dtype)
    tail = buf[tl:tl + CV_HIST, :]
    buf[0:CV_HIST, :] = tail
    tail_ref[0] = tail


def _cconv(h, hist, p, *, nb, seqlen, tl, row0, out_rows, orow0=0):
    rb0 = row0 // tl
    ob0 = orow0 // tl
    nt = seqlen // tl
    hspec = lambda cb: pl.BlockSpec((tl, D_GROUP), lambda b, t, cb=cb: (rb0 + b * nt + t, cb))
    full = lambda shape: pl.BlockSpec(shape, lambda b, t: (0,) * len(shape))
    kern = functools.partial(_cconv_kernel, tl=tl)
    return pl.pallas_call(
        kern,
        grid=(nb, nt),
        in_specs=[hspec(CB_CA), hspec(CB_CG), pl.BlockSpec((1, CV_HIST, D_GROUP), lambda b, t: (b, 0, 0)),
                  full((CV_HIST, D_GROUP)), full((1, D_GROUP)), full((1, D_GROUP)), full((1, D_GROUP))],
        out_specs=[pl.BlockSpec((tl, D_GROUP), lambda b, t: (ob0 + b * nt + t, 0)),
                   pl.BlockSpec((1, CV_HIST, D_GROUP), lambda b, t: (b, 0, 0))],
        out_shape=[jax.ShapeDtypeStruct((out_rows, D_GROUP), BF16),
                   jax.ShapeDtypeStruct((nb, CV_HIST, D_GROUP), F32)],
        scratch_shapes=[pltpu.VMEM((tl + CV_HIST, D_GROUP), F32),
                        pltpu.VMEM((SUBLANES - 1, tl + CV_HIST - SUBLANES, D_GROUP), F32)],
        compiler_params=_cp(("parallel", "arbitrary")),
        name=p["name"],
    )(h, h, hist, p["w"], p["b"], p["lg"], p["lb"])


R_ROWS = 32


def _outproj_kernel(*refs, n_full, rem):
    yp_refs, ys_refs = refs[0:4], refs[4:8]
    (x_ref, w_ref, g_ref, b_ref, wr_ref, br_ref, x1_ref, gcol_ref, eid_ref, cnt_ref, run_ref, ym_ref) = refs[8:]
    i = pl.program_id(0)
    tm = x_ref.shape[0]

    @pl.when(i == 0)
    def _():
        run_ref[...] = jnp.zeros(run_ref.shape, F32)

    @pl.when(i < n_full)
    def _():
        for m in range(4):
            ym_ref[m] = yp_refs[m][...]

    @pl.when(i >= n_full)
    def _():
        for m in range(4):
            if rem:
                ym_ref[m, 0:rem, :] = yp_refs[m][0:rem, :]
            ym_ref[m, rem:tm, :] = ys_refs[m][...]

    acc = jnp.dot(ym_ref[0], w_ref[0], preferred_element_type=F32)
    for m in range(1, 4):
        acc = acc + jnp.dot(ym_ref[m], w_ref[m], preferred_element_type=F32)
    x1 = _ln(ALPHA * x_ref[...] + acc, g_ref[...], b_ref[...])
    x1_ref[...] = x1
    lt = lax.dot_general(wr_ref[...], x1, NT, precision=HI, preferred_element_type=F32) + br_ref[...]
    lg = lt[0:N_EGROUPS, :]
    ridx = lax.broadcasted_iota(jnp.int32, (N_EGROUPS, tm), 0).astype(F32)

    def softmax0(a):
        e = jnp.exp(a - jnp.max(a, 0, keepdims=True))
        return e / jnp.sum(e, 0, keepdims=True)

    def top1(pv):
        best = jnp.max(pv, 0, keepdims=True)
        return best, jnp.min(jnp.where(pv == best, ridx, float(N_EGROUPS)), 0, keepdims=True)

    p_g, g_idx = top1(softmax0(lg))
    le = jnp.zeros((EXP_PER_GROUP, tm), F32)
    for g in range(N_EGROUPS):
        r0 = SUBLANES + g * EXP_PER_GROUP
        le = le + jnp.where(g_idx == g, lt[r0:r0 + EXP_PER_GROUP, :], 0.0)
    pe = softmax0(le)
    p1, i1 = top1(pe)
    p2, i2 = top1(jnp.where(ridx == i1, -1.0, pe))
    den = p1 + p2
    e1 = g_idx * EXP_PER_GROUP + i1
    e2 = g_idx * EXP_PER_GROUP + i2
    srow = lax.broadcasted_iota(jnp.int32, (SUBLANES, tm), 0)
    gates8 = jnp.where(srow == 0, p_g * p1 / den, jnp.where(srow == 1, p_g * p2 / den, 0.0))
    eye8 = (lax.broadcasted_iota(jnp.int32, (SUBLANES, LANES), 0)
            == lax.broadcasted_iota(jnp.int32, (SUBLANES, LANES), 1)).astype(F32)
    gcol_ref[...] = lax.dot_general(gates8, eye8, TN, precision=HI, preferred_element_type=F32)
    eio = lax.broadcasted_iota(jnp.int32, (N_EXPERTS, tm), 0).astype(F32)
    sel0 = eio == e1
    sel1 = eio == e2
    before = (lax.broadcasted_iota(jnp.int32, (tm, tm), 0)
              < lax.broadcasted_iota(jnp.int32, (tm, tm), 1))
    before = jnp.where(before, 1.0, 0.0).astype(BF16)
    oh0 = jnp.where(sel0, 1.0, 0.0)
    oh1 = jnp.where(sel1, 1.0, 0.0)
    cum0 = jnp.dot(oh0.astype(BF16), before, preferred_element_type=F32)
    cum1 = jnp.dot(oh1.astype(BF16), before, preferred_element_type=F32)
    tot0 = jnp.sum(oh0, 1, keepdims=True)
    tot1 = jnp.sum(oh1, 1, keepdims=True)
    base = run_ref[...]
    r0 = jnp.sum(jnp.where(sel0, base + cum0, 0.0), 0, keepdims=True)
    r1 = jnp.sum(jnp.where(sel1, base + tot0 + cum1, 0.0), 0, keepdims=True)
    run_ref[...] = base + tot0 + tot1
    cnt_ref[...] = jnp.broadcast_to(base + tot0 + tot1, cnt_ref.shape)
    eid_ref[...] = jnp.zeros(eid_ref.shape, jnp.int32)
    eid_ref[0:1, :] = e1.astype(jnp.int32)
    eid_ref[1:2, :] = e2.astype(jnp.int32)
    eid_ref[2:3, :] = r0.astype(jnp.int32)
    eid_ref[3:4, :] = r1.astype(jnp.int32)


def _outproj(ys_prompt, ys_sample, x, w_out, g, b, wr, br):
    t = x.shape[0]
    tp, ts = ys_prompt[0].shape[0], ys_sample[0].shape[0]
    tm = _pick(t, (640, 128))
    n_full, rem = divmod(tp, tm)
    assert rem + ts == tm and t == tp + ts, "sample rows must complete the last token tile"
    last_pb = pl.cdiv(tp, tm) - 1
    yspec = pl.BlockSpec((tm, D_GROUP), lambda i: (jnp.minimum(i, last_pb), 0))
    sspec = pl.BlockSpec((ts, D_GROUP), lambda i: (0, 0))
    full = lambda shape: pl.BlockSpec(shape, lambda i: (0,) * len(shape))
    kern = functools.partial(_outproj_kernel, n_full=n_full, rem=rem)
    return pl.pallas_call(
        kern,
        grid=(t // tm,),
        in_specs=[yspec, yspec, yspec, yspec, sspec, sspec, sspec, sspec,
                  pl.BlockSpec((tm, D_MODEL), lambda i: (i, 0)),
                  full((4, D_GROUP, D_MODEL)), full((1, D_MODEL)), full((1, D_MODEL)),
                  full((R_ROWS, D_MODEL)), full((R_ROWS, 1))],
        out_specs=[pl.BlockSpec((tm, D_MODEL), lambda i: (i, 0)),
                   pl.BlockSpec((tm, LANES), lambda i: (i, 0)),
                   pl.BlockSpec((SUBLANES, tm), lambda i: (0, i)),
                   full((N_EXPERTS, LANES))],
        out_shape=[jax.ShapeDtypeStruct((t, D_MODEL), F32), jax.ShapeDtypeStruct((t, LANES), F32),
                   jax.ShapeDtypeStruct((SUBLANES, t), jnp.int32),
                   jax.ShapeDtypeStruct((N_EXPERTS, LANES), F32)],
        scratch_shapes=[pltpu.VMEM((N_EXPERTS, 1), F32), pltpu.VMEM((4, tm, D_GROUP), BF16)],
        compiler_params=_cp(("arbitrary",)),
        name="outproj_ln_router",
    )(*ys_prompt, *ys_sample, x, w_out, g, b, wr, br)


def _moe_kernel(te_ref, nu_ref, x_ref, w1_ref, w3_ref, w2_ref, y_ref, w1b, w3b, w2b):
    i = pl.program_id(0)
    n_used = nu_ref[0]

    @pl.when(i < n_used)
    def _():
        @pl.when((i == 0) | (te_ref[i] != te_ref[jnp.maximum(i - 1, 0)]))
        def _():
            w1b[...] = w1_ref[0].astype(BF16)
            w3b[...] = w3_ref[0].astype(BF16)
            w2b[...] = w2_ref[0].astype(BF16)

        x = x_ref[...].astype(BF16)
        h1 = jnp.dot(x, w1b[...], preferred_element_type=F32)
        h3 = jnp.dot(x, w3b[...], preferred_element_type=F32)
        hh = (_silu(h1) * h3).astype(BF16)
        y_ref[...] = jnp.dot(hh, w2b[...], preferred_element_type=F32)

    @pl.when(i >= n_used)
    def _():
        y_ref[...] = jnp.zeros(y_ref.shape, F32)


def _moe(tile_expert, n_used, x_sorted, w1, w3, w2, *, n_tiles, layer):
    tm = MOE_TM
    xmap = lambda i, te, nu: (jnp.maximum(jnp.minimum(i, nu[0] - 1), 0), 0)
    wmap = lambda i, te, nu: (layer * N_EXPERTS + te[i], 0, 0)
    grid_spec = pltpu.PrefetchScalarGridSpec(
        num_scalar_prefetch=2,
        grid=(n_tiles,),
        in_specs=[pl.BlockSpec((tm, D_MODEL), xmap),
                  pl.BlockSpec((1, D_MODEL, D_EXPERT), wmap),
                  pl.BlockSpec((1, D_MODEL, D_EXPERT), wmap),
                  pl.BlockSpec((1, D_EXPERT, D_MODEL), wmap)],
        out_specs=pl.BlockSpec((tm, D_MODEL), lambda i, te, nu: (i, 0)),
        scratch_shapes=[pltpu.VMEM((D_MODEL, D_EXPERT), BF16), pltpu.VMEM((D_MODEL, D_EXPERT), BF16),
                        pltpu.VMEM((D_EXPERT, D_MODEL), BF16)],
    )
    return pl.pallas_call(
        _moe_kernel,
        grid_spec=grid_spec,
        out_shape=jax.ShapeDtypeStruct((n_tiles * tm, D_MODEL), F32),
        compiler_params=_cp(("arbitrary",)),
        name="moe_grouped",
    )(tile_expert, n_used, x_sorted, w1, w3, w2)


def _disp_row_copy(x_ref, xs_hbm, sem, r, row):
    return pltpu.make_async_copy(x_ref.at[pl.ds(r, 1), :], xs_hbm.at[pl.ds(row, 1), :], sem)


def _disp_tile_copy(zbuf, xs_hbm, sem, tile, tm):
    return pltpu.make_async_copy(zbuf, xs_hbm.at[pl.ds(pl.multiple_of(tile * tm, tm), tm), :], sem)


def _dispatch_kernel(pos_ref, plo_ref, phi_ref, nu_ref, x_ref, xs_hbm, zbuf, sem, zsem, tsem, *, tm, t, mt, n_tiles):
    i = pl.program_id(0)

    def body(r, c):
        for k in range(2):
            _disp_row_copy(x_ref, xs_hbm, sem, r, pos_ref[k * t + i * tm + r]).start(priority=k)
        return c
    lax.fori_loop(0, tm, body, 0, unroll=DMA_UNROLL)

    @pl.when(i == 0)
    def _():
        zbuf[...] = jnp.zeros(zbuf.shape, F32)
        for e in range(N_EXPERTS):
            def zbody(row, c):
                _disp_row_copy(zbuf, xs_hbm, zsem, 0, row).start()
                return c
            lax.fori_loop(plo_ref[e], phi_ref[e], zbody, 0)

        def tbody(tile, c):
            _disp_tile_copy(zbuf, xs_hbm, tsem, tile, mt).start()
            return c
        lax.fori_loop(nu_ref[0], n_tiles, tbody, 0)
        for e in range(N_EXPERTS):
            def zwait(row, c):
                _disp_row_copy(zbuf, xs_hbm, zsem, 0, row).wait()
                return c
            lax.fori_loop(plo_ref[e], phi_ref[e], zwait, 0)

        def twait(tile, c):
            _disp_tile_copy(zbuf, xs_hbm, tsem, tile, mt).wait()
            return c
        lax.fori_loop(nu_ref[0], n_tiles, twait, 0)

    def wbody(r, c):
        for k in range(2):
            _disp_row_copy(x_ref, xs_hbm, sem, r, 0).wait()
        return c
    lax.fori_loop(0, tm, wbody, 0, unroll=DMA_UNROLL)


def _dispatch(pos, pad_lo, pad_hi, n_used, x1, *, n_tiles):
    t = x1.shape[0]
    tm = _pick(t, (640, 128))
    mt = MOE_TM
    kern = functools.partial(_dispatch_kernel, tm=tm, t=t, mt=mt, n_tiles=n_tiles)
    grid_spec = pltpu.PrefetchScalarGridSpec(
        num_scalar_prefetch=4,
        grid=(t // tm,),
        in_specs=[pl.BlockSpec((tm, D_MODEL), lambda i, *_: (i, 0))],
        out_specs=pl.BlockSpec(memory_space=pl.ANY),
        scratch_shapes=[pltpu.VMEM((mt, D_MODEL), F32), pltpu.SemaphoreType.DMA(()),
                        pltpu.SemaphoreType.DMA(()), pltpu.SemaphoreType.DMA(())],
    )
    return pl.pallas_call(
        kern,
        grid_spec=grid_spec,
        out_shape=jax.ShapeDtypeStruct((n_tiles * mt, D_MODEL), F32),
        compiler_params=_cp(("arbitrary",)),
        name="moe_dispatch",
    )(pos, pad_lo, pad_hi, n_used, x1)


def _moe_plan(ids, cnt, n_tiles):
    tm = MOE_TM
    counts = cnt[:, 0].astype(jnp.int32)
    ntile_e = (counts + tm - 1) // tm
    tile_end = jnp.cumsum(ntile_e)
    row_start = (tile_end - ntile_e) * tm
    n_used = tile_end[-1]
    tiles = jnp.arange(n_tiles, dtype=jnp.int32)
    te = jnp.sum((tiles[:, None] >= tile_end[None, :]).astype(jnp.int32), 1)
    last_e = jnp.sum((n_used - 1 >= tile_end).astype(jnp.int32))
    te = jnp.where(tiles < n_used, te, last_e).astype(jnp.int32)
    eid, rank = ids[0:2], ids[2:4]
    sel = eid[:, :, None] == jnp.arange(N_EXPERTS, dtype=jnp.int32)
    pos = (jnp.sum(jnp.where(sel, row_start, 0), -1) + rank).reshape(-1).astype(jnp.int32)
    pad_lo = (row_start + counts).astype(jnp.int32)
    pad_hi = (row_start + ntile_e * tm).astype(jnp.int32)
    return te, n_used.reshape(1).astype(jnp.int32), pos, pad_lo, pad_hi


def _comb_row_copy(y_hbm, ybuf, sem, row, slot, k, r):
    return pltpu.make_async_copy(y_hbm.at[pl.ds(row, 1), :], ybuf.at[slot, k, pl.ds(r, 1), :], sem.at[slot])


def _combine_kernel(pos_ref, y_hbm, x1_ref, gc_ref, g_ref, b_ref, xf_ref, xb_ref, ybuf, sem, *, tm, t):
    i = pl.program_id(0)
    n = pl.num_programs(0)

    def start_tile(tile, slot):
        def body(r, c):
            for k in range(2):
                _comb_row_copy(y_hbm, ybuf, sem, pos_ref[k * t + tile * tm + r], slot, k, r).start(priority=k)
            return c
        lax.fori_loop(0, tm, body, 0, unroll=DMA_UNROLL)

    @pl.when(i == 0)
    def _():
        start_tile(0, 0)

    @pl.when(i + 1 < n)
    def _():
        start_tile(i + 1, (i + 1) % 2)

    slot = i % 2

    def wbody(r, c):
        for k in range(2):
            _comb_row_copy(y_hbm, ybuf, sem, 0, slot, k, r).wait()
        return c
    lax.fori_loop(0, tm, wbody, 0, unroll=DMA_UNROLL)
    ffn = gc_ref[:, 0:1] * ybuf[slot, 0] + gc_ref[:, 1:2] * ybuf[slot, 1]
    x2 = _ln(ALPHA * x1_ref[...] + ffn, g_ref[...], b_ref[...])
    xf_ref[...] = x2
    xb_ref[...] = x2.astype(BF16)


def _combine(pos, y_sorted, x1, gcol, g, b):
    t = x1.shape[0]
    tm = _pick(t, (320, 128))
    kern = functools.partial(_combine_kernel, tm=tm, t=t)
    grid_spec = pltpu.PrefetchScalarGridSpec(
        num_scalar_prefetch=1,
        grid=(t // tm,),
        in_specs=[pl.BlockSpec(memory_space=pl.ANY),
                  pl.BlockSpec((tm, D_MODEL), lambda i, pos: (i, 0)),
                  pl.BlockSpec((tm, LANES), lambda i, pos: (i, 0)),
                  pl.BlockSpec((1, D_MODEL), lambda i, pos: (0, 0)),
                  pl.BlockSpec((1, D_MODEL), lambda i, pos: (0, 0))],
        out_specs=[pl.BlockSpec((tm, D_MODEL), lambda i, pos: (i, 0)),
                   pl.BlockSpec((tm, D_MODEL), lambda i, pos: (i, 0))],
        scratch_shapes=[pltpu.VMEM((2, 2, tm, D_MODEL), F32), pltpu.SemaphoreType.DMA((2,))],
    )
    return pl.pallas_call(
        kern,
        grid_spec=grid_spec,
        out_shape=[jax.ShapeDtypeStruct((t, D_MODEL), F32), jax.ShapeDtypeStruct((t, D_MODEL), BF16)],
        compiler_params=_cp(("arbitrary",)),
        name="moe_combine_ln",
    )(pos, y_sorted, x1, gcol, g, b)


def _row(v, n=None):
    v = v.reshape(1, -1).astype(F32)
    if n is not None and v.shape[1] < n:
        v = jnp.pad(v, ((0, 0), (0, n - v.shape[1])))
    return v


def kernel(x_prompt, x_sample, state_ssm_conv, state_ssm, cache_k, cache_v, state_conv, ln_in_g, ln_in_b, w_in, ssm_conv_w, ssm_conv_b, ssm_dt_bias, ssm_a_log, ssm_d, ssm_norm_g, da_lq1, da_lk1, da_lq2, da_lk2, da_norm_g, sg_ln_g, sg_ln_b, sg_w, sg_b, cv_w, cv_b, cv_ln_g, cv_ln_b, w_out, ln1_g, ln1_b, moe_wg_group, moe_bg_group, moe_wg_exp, moe_bg_exp, moe_w1, moe_w3, moe_w2, ln2_g, ln2_b):
    bp, tp, _ = x_prompt.shape
    nb, ls, _ = x_sample.shape
    depth = w_in.shape[0]
    past = cache_k.shape[2]
    assert bp == 1 and tp % 512 == 0 and ls % SUBLANES == 0 and ls <= CHUNK
    ts = nb * ls
    t = tp + ts
    n_tiles = (2 * t + N_EXPERTS * (MOE_TM - 1)) // MOE_TM + 1

    dt0 = D_GROUP + SSM_CONV_CH
    w1s = moe_w1.reshape(depth * N_EXPERTS, D_MODEL, D_EXPERT)
    w3s = moe_w3.reshape(depth * N_EXPERTS, D_MODEL, D_EXPERT)
    w2s = moe_w2.reshape(depth * N_EXPERTS, D_EXPERT, D_MODEL)
    wr = jnp.zeros((depth, R_ROWS, D_MODEL), F32)
    wr = wr.at[:, 0:N_EGROUPS].set(jnp.swapaxes(moe_wg_group, 1, 2))
    wr = wr.at[:, SUBLANES:SUBLANES + N_EXPERTS].set(jnp.swapaxes(moe_wg_exp, 1, 2))
    br = jnp.zeros((depth, R_ROWS, 1), F32)
    br = br.at[:, 0:N_EGROUPS, 0].set(moe_bg_group).at[:, SUBLANES:SUBLANES + N_EXPERTS, 0].set(moe_bg_exp)
    cache_k2 = cache_k.reshape(depth * nb, past, D_GROUP)
    cache_v2 = cache_v.reshape(depth * nb, past, D_GROUP)

    tables_p = _rope_tables(np.arange(tp))
    tables_s = _rope_tables(np.tile(past + np.arange(ls), nb))

    xf, xb = _ln_in(x_prompt.reshape(tp, D_MODEL), x_sample.reshape(ts, D_MODEL), ln_in_g, ln_in_b)

    zeros_hist = jnp.zeros((1, SUBLANES, SSM_CONV_CH), F32)
    zeros_state = jnp.zeros((1, SSM_STATE, D_GROUP), F32)
    zeros_cv = jnp.zeros((1, CV_HIST, D_GROUP), F32)
    outs = {k: [] for k in ("p_sc", "p_ss", "p_k", "p_v", "p_cv", "s_sc", "s_ss", "s_k", "s_v", "s_cv", "s_sg")}
    tl_p = 512
    tl_c = 256

    for l in range(depth):
        lam_init = 0.8 - 0.6 * math.exp(-0.3 * l)
        w_in_l = w_in[l]
        w_main = jnp.concatenate([w_in_l[:, :dt0], w_in_l[:, dt0 + SSM_HEADS:]], -1).astype(BF16)
        w_dt = jnp.pad(w_in_l[:, dt0:dt0 + SSM_HEADS], ((0, 0), (0, LANES - SSM_HEADS))).astype(BF16)
        w_out_b = w_out[l].reshape(4, D_GROUP, D_MODEL).astype(BF16)
        h, dt = _inproj(xb, w_main, w_dt)

        ssd_p = dict(cw=ssm_conv_w[l], cb=_row(ssm_conv_b[l]), dtb=_row(ssm_dt_bias[l], LANES),
                     a=_row(-jnp.exp(ssm_a_log[l].astype(F32)), LANES),
                     dsk=_row(jnp.repeat(ssm_d[l], SSM_HEAD_DIM)), ng=_row(ssm_norm_g[l]))
        ya, hist_p, st_p = _ssd(h, dt, zeros_hist, zeros_state, dict(ssd_p, len=tp, name="ssd_prompt"),
                                nb=1, tl=tl_p, q=CHUNK, row0=0, out_rows=tp)
        hist_s_in = jnp.pad(state_ssm_conv[l], ((0, 0), (SUBLANES - (SSM_CONV - 1), 0), (0, 0)))
        h0t_s = jnp.swapaxes(state_ssm[l].reshape(nb, D_GROUP, SSM_STATE), 1, 2)
        ya_s, hist_s, st_s = _ssd(h, dt, hist_s_in, h0t_s, dict(ssd_p, len=ls, name="ssd_sample"),
                                  nb=nb, tl=ls, q=ls, row0=tp, out_rows=ts)
        outs["p_sc"].append(hist_p[:, SUBLANES - (SSM_CONV - 1):])
        outs["s_sc"].append(hist_s[:, SUBLANES - (SSM_CONV - 1):])
        unt = lambda s: jnp.swapaxes(s, 1, 2).reshape(-1, SSM_HEADS, SSM_HEAD_DIM, SSM_STATE)
        outs["p_ss"].append(unt(st_p))
        outs["s_ss"].append(unt(st_s))

        qb, kf, kb, vb, vf = _rope(h, tables_p, row0=0, nrows=tp)
        qb_s, kf_s, kb_s, vb_s, vf_s = _rope(h, tables_s, row0=tp, nrows=ts)
        f32 = lambda a: a.astype(F32)
        lam = (jnp.exp(jnp.sum(f32(da_lq1[l]) * f32(da_lk1[l]))) - jnp.exp(jnp.sum(f32(da_lq2[l]) * f32(da_lk2[l])))
               + lam_init).reshape(1).astype(F32)
        gda = _row(da_norm_g[l])
        yb = _attn_prompt(lam, qb, kb, vb, gda, tp=tp, out_rows=tp, scale_out=1.0 - lam_init)
        yb_s = _attn_sample(lam, qb_s, kb_s, vb_s, cache_k2, cache_v2, gda, layer=l, nb=nb, ls=ls, row0=0,
                            scale_out=1.0 - lam_init)
        outs["p_k"].append(kf.reshape(1, tp, DA_HEADS, 2, DA_DIM))
        outs["s_k"].append(kf_s.reshape(nb, ls, DA_HEADS, 2, DA_DIM))
        outs["p_v"].append(vf.reshape(1, tp, DA_HEADS, DA_VDIM))
        outs["s_v"].append(vf_s.reshape(nb, ls, DA_HEADS, DA_VDIM))

        sg_common = dict(g=_row(sg_ln_g[l]), b=_row(sg_ln_b[l]))
        bias_full = lambda qq: jnp.repeat(sg_b[l][:, :qq].T, LANES, axis=1).astype(F32)
        (yc,) = _sgu(h, dict(sg_common, w=sg_w[l], bias=bias_full(SG_CHUNK), name="sgu_prompt"),
                     nrows=tp, tl=tl_p, q=SG_CHUNK, row0=0, out_rows=tp, want_v=False)
        yc_s, v_rows = _sgu(h, dict(sg_common, w=sg_w[l][:, :ls, :ls], bias=bias_full(ls), name="sgu_sample"),
                            nrows=ts, tl=ls, q=ls, row0=tp, out_rows=ts, want_v=True)
        outs["s_sg"].append(v_rows.reshape(nb, ls, D_GROUP))

        cv_p = dict(w=jnp.pad(cv_w[l], ((0, CV_HIST - CV_WIDTH), (0, 0))), b=_row(cv_b[l]),
                    lg=_row(cv_ln_g[l]), lb=_row(cv_ln_b[l]))
        yd, tail_p = _cconv(h, zeros_cv, dict(cv_p, name="cconv_prompt"), nb=1, seqlen=tp, tl=tl_c, row0=0,
                            out_rows=tp)
        cv_hist_s = jnp.pad(state_conv[l], ((0, 0), (CV_HIST - (CV_WIDTH - 1), 0), (0, 0)))
        yd_s, tail_s = _cconv(h, cv_hist_s, dict(cv_p, name="cconv_sample"), nb=nb, seqlen=ls, tl=ls, row0=tp,
                              out_rows=ts)
        outs["p_cv"].append(tail_p[:, CV_HIST - (CV_WIDTH - 1):])
        outs["s_cv"].append(tail_s[:, CV_HIST - (CV_WIDTH - 1):])

        x1, gcol, ids, cnt = _outproj((ya, yb, yc, yd), (ya_s, yb_s, yc_s, yd_s), xf, w_out_b, _row(ln1_g[l]), _row(ln1_b[l]), wr[l], br[l])
        te, n_used, pos, pad_lo, pad_hi = _moe_plan(ids, cnt, n_tiles)
        x_sorted = _dispatch(pos, pad_lo, pad_hi, n_used, x1, n_tiles=n_tiles)
        y_sorted = _moe(te, n_used, x_sorted, w1s, w3s, w2s, n_tiles=n_tiles, layer=l)
        xf, xb = _combine(pos, y_sorted, x1, gcol, _row(ln2_g[l]), _row(ln2_b[l]))

    st = lambda k: jnp.stack(outs[k])
    return (xf[:tp].reshape(1, tp, D_MODEL), xf[tp:].reshape(nb, ls, D_MODEL),
            st("p_sc"), st("p_ss"), st("p_k"), st("p_v"), st("p_cv"),
            st("s_sc"), st("s_ss"), st("s_k"), st("s_v"), st("s_cv"), st("s_sg"))
```
